```python
import math, functools
import jax, jax.numpy as jnp
from jax import lax
import numpy as np

D_MODEL = 1024
BATCH = 16
SEQ = 2048
DEPTH = 2
DEC_BATCH = 32
DEC_SEQ = 8
PAST_LEN = 16384
PAGE_SIZE = 128

MIX_WIDTH = D_MODEL
ATTN_WIDTH = MIX_WIDTH // 2
CONV_CH = MIX_WIDTH - ATTN_WIDTH
N_HEADS = 8
HEAD_DIM = ATTN_WIDTH // N_HEADS
MOBA_BLOCK = 256
MOBA_TOPK = 3
CONV_K = 31
ROPE_THETA = 10000.0
Q_CHUNK = 128
NORM_EPS = 1e-6
IN_COLS = 4 * ATTN_WIDTH + 3 * CONV_CH

kernel_name = 'hymba_moba_conformer_decoder_step'


def rmsnorm(x, g):
    xf = x.astype(jnp.float32)
    y = xf * lax.rsqrt(jnp.mean(xf * xf, axis=-1, keepdims=True) + NORM_EPS)
    return (y * g.astype(jnp.float32)).astype(x.dtype)


def layernorm(x, g, b):
    xf = x.astype(jnp.float32)
    mu = jnp.mean(xf, axis=-1, keepdims=True)
    var = jnp.mean(jnp.square(xf - mu), axis=-1, keepdims=True)
    y = (xf - mu) * lax.rsqrt(var + NORM_EPS) * g.astype(jnp.float32) + b.astype(jnp.float32)
    return y.astype(x.dtype)


def rope(x, pos):
    half = HEAD_DIM // 2
    inv = ROPE_THETA ** (-jnp.arange(half, dtype=jnp.float32) * (2.0 / HEAD_DIM))
    ang = pos.astype(jnp.float32)[:, None] * inv[None, :]
    cos = jnp.cos(ang)[:, None, :]
    sin = jnp.sin(ang)[:, None, :]
    xf = x.astype(jnp.float32)
    x1, x2 = xf[..., :half], xf[..., half:]
    return jnp.concatenate([x1 * cos - x2 * sin, x2 * cos + x1 * sin], axis=-1).astype(x.dtype)


def blockify(k):
    L = k.shape[-3]
    nb = max(-(-L // MOBA_BLOCK), MOBA_TOPK)
    pad = [(0, 0)] * (k.ndim - 3) + [(0, nb * MOBA_BLOCK - L), (0, 0), (0, 0)]
    kp = jnp.pad(k, pad)
    return kp.reshape(k.shape[:-3] + (nb, MOBA_BLOCK) + k.shape[-2:])


def moba_attend(q, q_pos, kb, vb, km):
    Q, H, D = q.shape
    nb = kb.shape[0]
    scale = 1.0 / math.sqrt(D)
    qf = q.astype(jnp.float32)
    bq = q_pos // MOBA_BLOCK
    gate = jnp.einsum('qhd,nhd->qhn', qf, km)
    past = jnp.arange(nb)[None, None, :] < bq[:, None, None]
    gate = jnp.where(past, gate, -jnp.inf)
    _, idx = lax.top_k(gate, MOBA_TOPK)
    valid = idx < bq[:, None, None]
    kbh = jnp.transpose(kb, (2, 0, 1, 3))
    vbh = jnp.transpose(vb, (2, 0, 1, 3))
    hidx = jnp.arange(H)[None, :, None]
    ksel = kbh[hidx, idx].astype(jnp.float32)
    vsel = vbh[hidx, idx].astype(jnp.float32)
    kown = kbh[jnp.arange(H)[None, :], bq[:, None]].astype(jnp.float32)
    vown = vbh[jnp.arange(H)[None, :], bq[:, None]].astype(jnp.float32)
    s_sel = jnp.einsum('qhd,qhspd->qhsp', qf, ksel) * scale
    s_sel = jnp.where(valid[..., None], s_sel, -jnp.inf)
    own_pos = bq[:, None] * MOBA_BLOCK + jnp.arange(MOBA_BLOCK)[None, :]
    causal = own_pos <= q_pos[:, None]
    s_own = jnp.einsum('qhd,qhpd->qhp', qf, kown) * scale
    s_own = jnp.where(causal[:, None, :], s_own, -jnp.inf)
    s = jnp.concatenate([s_sel.reshape(Q, H, MOBA_TOPK * MOBA_BLOCK), s_own], axis=-1)
    p = jax.nn.softmax(s, axis=-1)
    p_sel = p[..., :MOBA_TOPK * MOBA_BLOCK].reshape(Q, H, MOBA_TOPK, MOBA_BLOCK)
    p_own = p[..., MOBA_TOPK * MOBA_BLOCK:]
    o = jnp.einsum('qhsp,qhspd->qhd', p_sel, vsel) + jnp.einsum('qhp,qhpd->qhd', p_own, vown)
    return o.astype(q.dtype)


def moba_prompt(q, k, v, pos):
    B, S, H, D = q.shape
    kb = blockify(k)
    vb = blockify(v)
    km = jnp.mean(kb.astype(jnp.float32), axis=-3)
    qc = q.reshape(B, S // Q_CHUNK, Q_CHUNK, H, D)
    pc = pos.reshape(S // Q_CHUNK, Q_CHUNK)

    def per_seq(a):
        qs, kbs, vbs, kms = a
        return lax.map(lambda b: moba_attend(b[0], b[1], kbs, vbs, kms), (qs, pc))

    out = lax.map(per_seq, (qc, kb, vb, km))
    return out.reshape(B, S, H, D)


def moba_sample(q, k, v, pos, cache_k, cache_v, page_table, layer):
    def per_seq(a):
        pt, qs, ks, vs = a
        kpast = cache_k[layer, pt].reshape(-1, N_HEADS, HEAD_DIM).astype(ks.dtype)
        vpast = cache_v[layer, pt].reshape(-1, N_HEADS, HEAD_DIM).astype(vs.dtype)
        kb = blockify(jnp.concatenate([kpast, ks], axis=0))
        vb = blockify(jnp.concatenate([vpast, vs], axis=0))
        km = jnp.mean(kb.astype(jnp.float32), axis=-3)
        return moba_attend(qs, pos, kb, vb, km)

    return lax.map(per_seq, (page_table, q, k, v))


def causal_dwconv(xp, w, b):
    y = lax.conv_general_dilated(xp, w.astype(xp.dtype)[:, None, :], (1,), 'VALID',
                                 dimension_numbers=('NWC', 'WIO', 'NWC'),
                                 feature_group_count=xp.shape[-1])
    return y + b.astype(xp.dtype)


def mixer_layer(x, c, pos, conv_hist, attend, prm):
    (ng, wa, ba, wi, dw_w, dw_b, ln_g, ln_b, pw2, ag, cg, wo) = prm
    B, L, _ = x.shape
    mod = jax.nn.silu(c) @ wa + ba
    shift, scale, gate = jnp.split(mod[:, None, :], 3, axis=-1)
    h = rmsnorm(x, ng) * (1 + scale) + shift
    z = h @ wi
    A, C = ATTN_WIDTH, CONV_CH
    q, k, v, g_att, u, u_gate, g_conv = jnp.split(
        z, [A, 2 * A, 3 * A, 4 * A, 4 * A + C, 4 * A + 2 * C], axis=-1)
    q = rope(q.reshape(B, L, N_HEADS, HEAD_DIM), pos)
    k = rope(k.reshape(B, L, N_HEADS, HEAD_DIM), pos)
    v = v.reshape(B, L, N_HEADS, HEAD_DIM)
    att = attend(q, k, v, pos).reshape(B, L, A)
    glu = u * jax.nn.sigmoid(u_gate)
    xp = jnp.concatenate([conv_hist.astype(glu.dtype), glu], axis=1)
    dw = causal_dwconv(xp, dw_w, dw_b)
    co = jax.nn.silu(layernorm(dw, ln_g, ln_b)) @ pw2
    mixed = jnp.concatenate([rmsnorm(att, ag) * jax.nn.silu(g_att),
                             rmsnorm(co, cg) * jax.nn.silu(g_conv)], axis=-1)
    x = x + gate * (mixed @ wo)
    return x, k, v, xp[:, -(CONV_K - 1):]


def setup_inputs(seed: int = 0) -> dict:
    key = jax.random.key(seed)
    ks = jax.random.split(key, 24)
    n_pages = PAST_LEN // PAGE_SIZE
    used = DEC_BATCH * n_pages
    n_pool = used + max(1, used // 4)
    f32 = jnp.float32
    nrm = lambda k, s: jax.random.normal(k, s, f32)
    page_table = jax.random.permutation(ks[7], n_pool)[:used].reshape(DEC_BATCH, n_pages).astype(jnp.int32)
    return {
        'x_prompt': nrm(ks[0], (BATCH, SEQ, D_MODEL)),
        'x_sample': nrm(ks[1], (DEC_BATCH, DEC_SEQ, D_MODEL)),
        'c_prompt': nrm(ks[2], (BATCH, D_MODEL)),
        'c_sample': nrm(ks[3], (DEC_BATCH, D_MODEL)),
        'cache_k': nrm(ks[4], (DEPTH, n_pool, PAGE_SIZE, N_HEADS, HEAD_DIM)),
        'cache_v': nrm(ks[5], (DEPTH, n_pool, PAGE_SIZE, N_HEADS, HEAD_DIM)),
        'state_conv': 0.5 * nrm(ks[6], (DEPTH, DEC_BATCH, CONV_K - 1, CONV_CH)),
        'page_table': page_table,
        'norm_gain': 1.0 + 0.02 * nrm(ks[8], (DEPTH, D_MODEL)),
        'w_ada': nrm(ks[9], (DEPTH, D_MODEL, 3 * D_MODEL)) * (0.5 * D_MODEL ** -0.5),
        'b_ada': 0.02 * nrm(ks[10], (DEPTH, 3 * D_MODEL)),
        'w_in': nrm(ks[11], (DEPTH, D_MODEL, IN_COLS)) * D_MODEL ** -0.5,
        'conv_dw_w': nrm(ks[12], (DEPTH, CONV_K, CONV_CH)) * CONV_K ** -0.5,
        'conv_dw_b': 0.02 * nrm(ks[13], (DEPTH, CONV_CH)),
        'conv_ln_g': 1.0 + 0.02 * nrm(ks[14], (DEPTH, CONV_CH)),
        'conv_ln_b': 0.02 * nrm(ks[15], (DEPTH, CONV_CH)),
        'conv_pw2': nrm(ks[16], (DEPTH, CONV_CH, CONV_CH)) * CONV_CH ** -0.5,
        'attn_out_gain': 1.0 + 0.02 * nrm(ks[17], (DEPTH, ATTN_WIDTH)),
        'conv_out_gain': 1.0 + 0.02 * nrm(ks[18], (DEPTH, CONV_CH)),
        'w_out': nrm(ks[19], (DEPTH, MIX_WIDTH, D_MODEL)) * MIX_WIDTH ** -0.5,
        'final_gain': 1.0 + 0.02 * nrm(ks[20], (D_MODEL,)),
    }


def reference(x_prompt, x_sample, c_prompt, c_sample, cache_k, cache_v, state_conv, page_table,
              norm_gain, w_ada, b_ada, w_in, conv_dw_w, conv_dw_b, conv_ln_g, conv_ln_b,
              conv_pw2, attn_out_gain, conv_out_gain, w_out, final_gain):
    pos_p = jnp.arange(x_prompt.shape[1], dtype=jnp.int32)
    pos_s = PAST_LEN + jnp.arange(x_sample.shape[1], dtype=jnp.int32)
    hp, hs = x_prompt, x_sample
    kp_l, vp_l, cp_l, ks_l, vs_l, cs_l = [], [], [], [], [], []
    for l in range(DEPTH):
        prm = (norm_gain[l], w_ada[l], b_ada[l], w_in[l], conv_dw_w[l], conv_dw_b[l],
               conv_ln_g[l], conv_ln_b[l], conv_pw2[l], attn_out_gain[l], conv_out_gain[l], w_out[l])
        hist0 = jnp.zeros((hp.shape[0], CONV_K - 1, CONV_CH), hp.dtype)
        hp, kp, vp, cp = mixer_layer(hp, c_prompt, pos_p, hist0, moba_prompt, prm)
        attend_s = functools.partial(moba_sample, cache_k=cache_k, cache_v=cache_v,
                                     page_table=page_table, layer=l)
        hs, ksn, vsn, csn = mixer_layer(hs, c_sample, pos_s, state_conv[l], attend_s, prm)
        kp_l.append(kp); vp_l.append(vp); cp_l.append(cp)
        ks_l.append(ksn); vs_l.append(vsn); cs_l.append(csn)
    y_prompt = rmsnorm(hp, final_gain)
    y_sample = rmsnorm(hs, final_gain)
    return (y_prompt, y_sample, jnp.stack(kp_l), jnp.stack(vp_l), jnp.stack(cp_l),
            jnp.stack(ks_l), jnp.stack(vs_l), jnp.stack(cs_l))
```

```python
import functools
import math

import jax
import jax.numpy as jnp
from jax import lax
from jax.experimental import pallas as pl
from jax.experimental.pallas import tpu as pltpu

N_HEADS = 8
HEAD_DIM = 64
ATTN_WIDTH = N_HEADS * HEAD_DIM
MOBA_BLOCK = 256
MOBA_TOPK = 3
CONV_K = 31
CONV_HIST = CONV_K - 1
ROPE_THETA = 10000.0
NORM_EPS = 1e-6

LANES = 128
HIST_PAD = 32
NEG_BIG = -1e30
VMEM_LIMIT_BYTES = 56 * 1024 * 1024
PAGES_PER_STEP = 16

F32 = jnp.float32
BF16 = jnp.bfloat16


def _silu(x):
    return x * jax.nn.sigmoid(x)


def _dot(a, b):
    return jnp.dot(a, b, preferred_element_type=F32)


def _dot_nt(a, b):
    return lax.dot_general(a, b, (((1,), (1,)), ((), ())), preferred_element_type=F32)


def _params(*semantics):
    return pltpu.CompilerParams(dimension_semantics=semantics, vmem_limit_bytes=VMEM_LIMIT_BYTES)


def _ada_kernel(c_ref, wa_ref, ba_ref, o_ref):
    s = _silu(c_ref[...]).astype(BF16)
    o_ref[...] = _dot(s, wa_ref[...].astype(BF16)) + ba_ref[...]


def _ada(c_all, w_ada, b_ada):
    depth, d, n3 = w_ada.shape
    rows = c_all.shape[0]
    tn = 512
    return pl.pallas_call(
        _ada_kernel,
        grid=(depth, n3 // tn),
        in_specs=[
            pl.BlockSpec((rows, d), lambda l, n: (0, 0)),
            pl.BlockSpec((None, d, tn), lambda l, n: (l, 0, n)),
            pl.BlockSpec((None, 1, tn), lambda l, n: (l, 0, n)),
        ],
        out_specs=pl.BlockSpec((None, rows, tn), lambda l, n: (l, 0, n)),
        out_shape=jax.ShapeDtypeStruct((depth, rows, n3), F32),
        compiler_params=_params("arbitrary", "arbitrary"),
        name="ada",
    )(c_all, w_ada, b_ada.reshape(depth, 1, n3))


def _rope_tile(z, cos, sin_signed):
    lane = lax.broadcasted_iota(jnp.int32, (z.shape[0], LANES), 1)
    first_half = (lane % HEAD_DIM) < (HEAD_DIM // 2)
    outs = []
    for c in range(z.shape[1] // LANES):
        zc = z[:, c * LANES:(c + 1) * LANES]
        swapped = jnp.where(first_half,
                            pltpu.roll(zc, LANES - HEAD_DIM // 2, 1),
                            pltpu.roll(zc, HEAD_DIM // 2, 1))
        outs.append(zc * cos + swapped * sin_signed)
    return jnp.concatenate(outs, axis=1)


def _rope_tile_t(zt, cos_t, sin_t):
    half = HEAD_DIM // 2
    outs = []
    for h in range(N_HEADS):
        x1 = zt[h * HEAD_DIM:h * HEAD_DIM + half, :]
        x2 = zt[h * HEAD_DIM + half:(h + 1) * HEAD_DIM, :]
        outs.append(x1 * cos_t - x2 * sin_t)
        outs.append(x2 * cos_t + x1 * sin_t)
    return jnp.concatenate(outs, axis=0)


def _inproj_kernel(*refs, tm, rc, kv_t):
    (x_ref, mod_ref, ng_ref, wi_ref, cos_ref, sin_ref, hist_ref, dww_ref, dwb_ref, lng_ref, lnb_ref, pw2_ref,
     cg_ref) = refs[:13]
    refs = refs[13:]
    if kv_t:
        wkv_ref, cost_ref, sint_ref = refs[:3]
        q_ref, kb_ref, kt_ref, vt_ref, ga_ref, mc_ref, cs_ref, h_ref, xp_ref, gc_ref, sw_ref = refs[3:]
    else:
        q_ref, k_ref, v_ref, ga_ref, mc_ref, cs_ref, h_ref, xp_ref, gc_ref, sw_ref = refs
    a = ATTN_WIDTH
    j = pl.program_id(1)

    @pl.when(j == 0)
    def _():
        xp_ref[0:HIST_PAD - CONV_HIST, :] = jnp.zeros((HIST_PAD - CONV_HIST, a), F32)
        xp_ref[HIST_PAD - CONV_HIST:HIST_PAD, :] = hist_ref[...]

    x = x_ref[...]
    shift = mod_ref[0:1, :]
    scale = mod_ref[1:2, :]
    xn = x * lax.rsqrt(jnp.mean(x * x, axis=-1, keepdims=True) + NORM_EPS)
    h_ref[...] = ((xn * ng_ref[...]) * (1.0 + scale) + shift).astype(BF16)

    def proj(g):
        return _dot(h_ref[...], wi_ref[:, g * a:(g + 1) * a])

    cos = cos_ref[...]
    sin = sin_ref[...]
    q_ref[...] = (_rope_tile(proj(0), cos, sin) * (1.0 / math.sqrt(HEAD_DIM))).astype(BF16)
    if kv_t:
        kb_ref[...] = _rope_tile(proj(1), cos, sin).astype(BF16)
        kt_ref[...] = _rope_tile_t(_dot_nt(wkv_ref[0:a, :], h_ref[...]), cost_ref[...], sint_ref[...])
        vt_ref[...] = _dot_nt(wkv_ref[a:2 * a, :], h_ref[...])
    else:
        k_ref[...] = _rope_tile(proj(1), cos, sin)
        v_ref[...] = proj(2)
    ga_ref[...] = _silu(proj(3))
    xp_ref[HIST_PAD:HIST_PAD + tm, :] = proj(4) * jax.nn.sigmoid(proj(5))
    gc_ref[...] = _silu(proj(6))

    for r0 in range(0, tm, rc):
        acc = jnp.zeros((rc, a), F32) + dwb_ref[...]
        for tap in range(CONV_K):
            lo = r0 + (HIST_PAD - CONV_HIST) + tap
            acc = acc + xp_ref[lo:lo + rc, :] * dww_ref[tap:tap + 1, :]
        mu = jnp.mean(acc, axis=-1, keepdims=True)
        cen = acc - mu
        var = jnp.mean(cen * cen, axis=-1, keepdims=True)
        y = cen * lax.rsqrt(var + NORM_EPS) * lng_ref[...] + lnb_ref[...]
        sw_ref[r0:r0 + rc, :] = _silu(y).astype(BF16)

    co = _dot(sw_ref[...], pw2_ref[...])
    con = co * lax.rsqrt(jnp.mean(co * co, axis=-1, keepdims=True) + NORM_EPS) * cg_ref[...]
    mc_ref[...] = (con * gc_ref[...]).astype(BF16)

    cs_ref[...] = xp_ref[HIST_PAD + tm - CONV_HIST:HIST_PAD + tm, :]
    tail = xp_ref[tm:tm + HIST_PAD, :]
    xp_ref[0:HIST_PAD, :] = tail


def _inproj(x, mod3, ng, wi_bf, tables, hist, dww, dwb, lng, lnb, pw2_bf, cg, *, tm, wkv_t=None):
    b, s, d = x.shape
    a = ATTN_WIDTH
    n_in = wi_bf.shape[1]
    rc = min(tm, 32)
    kv_t = wkv_t is not None
    cos_t, sin_t, cos_tt, sin_tt = tables
    row = lambda v: v.reshape(1, -1)
    const = lambda shape: pl.BlockSpec(shape, lambda bi, j: (0,) * len(shape))
    tile = lambda w: pl.BlockSpec((None, tm, w), lambda bi, j: (bi, j, 0))
    tile_t = pl.BlockSpec((None, a, tm), lambda bi, j: (bi, 0, j))
    hist_spec = pl.BlockSpec((None, CONV_HIST, a), lambda bi, j: (bi, 0, 0))
    in_specs = [
        tile(d),
        pl.BlockSpec((None, 3, d), lambda bi, j: (bi, 0, 0)),
        const((1, d)),
        const((d, n_in)),
        pl.BlockSpec((tm, LANES), lambda bi, j: (j, 0)),
        pl.BlockSpec((tm, LANES), lambda bi, j: (j, 0)),
        hist_spec,
        const((CONV_K, a)),
        const((1, a)),
        const((1, a)),
        const((1, a)),
        const((a, a)),
        const((1, a)),
    ]
    args = [x, mod3, row(ng), wi_bf, cos_t, sin_t, hist, dww, row(dwb), row(lng), row(lnb), pw2_bf, row(cg)]
    q_shape = jax.ShapeDtypeStruct((b, s, a), BF16)
    tail_specs = [tile(a), tile(a), hist_spec]
    tail_shapes = [
        jax.ShapeDtypeStruct((b, s, a), F32),
        jax.ShapeDtypeStruct((b, s, a), BF16),
        jax.ShapeDtypeStruct((b, CONV_HIST, a), F32),
    ]
    if kv_t:
        half = HEAD_DIM // 2
        in_specs += [const((2 * a, d)),
                     pl.BlockSpec((half, tm), lambda bi, j: (0, j)),
                     pl.BlockSpec((half, tm), lambda bi, j: (0, j))]
        args += [wkv_t, cos_tt, sin_tt]
        out_specs = [tile(a), tile(a), tile_t, tile_t] + tail_specs
        out_shape = [q_shape,
                     jax.ShapeDtypeStruct((b, s, a), BF16),
                     jax.ShapeDtypeStruct((b, a, s), F32),
                     jax.ShapeDtypeStruct((b, a, s), F32),
                     ] + tail_shapes
    else:
        out_specs = [tile(a), tile(a), tile(a)] + tail_specs
        out_shape = [q_shape,
                     jax.ShapeDtypeStruct((b, s, a), F32),
                     jax.ShapeDtypeStruct((b, s, a), F32),
                     ] + tail_shapes
    return pl.pallas_call(
        functools.partial(_inproj_kernel, tm=tm, rc=rc, kv_t=kv_t),
        grid=(b, s // tm),
        in_specs=in_specs,
        out_specs=out_specs,
        out_shape=out_shape,
        scratch_shapes=[
            pltpu.VMEM((tm, d), BF16),
            pltpu.VMEM((HIST_PAD + tm, a), F32),
            pltpu.VMEM((tm, a), F32),
            pltpu.VMEM((tm, a), BF16),
        ],
        compiler_params=_params("arbitrary", "arbitrary"),
        name="inproj_t" if kv_t else "inproj",
    )(*args)


def _mix_out(att, ga, mc_bf, x, gate, ag, wo_ref, fg, final):
    a = ATTN_WIDTH
    an = att * lax.rsqrt(jnp.mean(att * att, axis=-1, keepdims=True) + NORM_EPS) * ag
    ma = (an * ga).astype(BF16)
    y = _dot(ma, wo_ref[0:a, :]) + _dot(mc_bf, wo_ref[a:2 * a, :])
    xn = x + gate * y
    if final:
        xn = xn * lax.rsqrt(jnp.mean(xn * xn, axis=-1, keepdims=True) + NORM_EPS) * fg
    return xn


def _outproj_kernel(att_ref, ga_ref, mc_ref, x_ref, mod_ref, ag_ref, wo_ref, fg_ref, o_ref, *, final):
    o_ref[...] = _mix_out(att_ref[...], ga_ref[...], mc_ref[...], x_ref[...], mod_ref[2:3, :], ag_ref[...], wo_ref,
                          fg_ref[...], final)


def _outproj(att, ga, mc, x, mod3, ag, wo_bf, fg, *, final):
    b, s, d = x.shape
    a = ATTN_WIDTH
    tile = lambda w: pl.BlockSpec((None, s, w), lambda bi: (bi, 0, 0))
    const = lambda shape: pl.BlockSpec(shape, lambda bi: (0,) * len(shape))
    return pl.pallas_call(
        functools.partial(_outproj_kernel, final=final),
        grid=(b,),
        in_specs=[tile(a), tile(a), tile(a), tile(d), pl.BlockSpec((None, 3, d), lambda bi: (bi, 0, 0)),
                  const((1, a)), const((2 * a, d)), const((1, d))],
        out_specs=tile(d),
        out_shape=jax.ShapeDtypeStruct((b, s, d), F32),
        compiler_params=_params("arbitrary"),
        name="outproj",
    )(att, ga, mc, x, mod3, ag.reshape(1, a), wo_bf, fg.reshape(1, d))


def _head_lane_mask(h, rows, width):
    lane = lax.broadcasted_iota(jnp.int32, (rows, width), 1)
    return (lane // HEAD_DIM) == h


def _moba_kernel(q_ref, kb_ref, vtf_ref, ga_ref, mc_ref, x_ref, mod_ref, ag_ref, wo_ref, fg_ref, o_ref,
                 vt_ref, km_ref, qm_ref, bias_ref, m_ref, l_ref, acc_ref, *, nb, final):
    blk = MOBA_BLOCK
    a = ATTN_WIDTH
    i = pl.program_id(1)

    @pl.when(i == 0)
    def _():
        for n in range(nb):
            vt_ref[n] = vtf_ref[:, n * blk:(n + 1) * blk].astype(BF16)
            mean_n = jnp.mean(kb_ref[n * blk:(n + 1) * blk, :].astype(F32), axis=0, keepdims=True)
            for h in range(N_HEADS):
                km_ref[h * nb + n:h * nb + n + 1, :] = jnp.where(_head_lane_mask(h, 1, a), mean_n, 0.0)

    q = q_ref[...]
    gate_t = _dot_nt(km_ref[...].astype(BF16), q)
    n_iota = lax.broadcasted_iota(jnp.int32, (nb, blk), 0)
    past = n_iota < i
    for h in range(N_HEADS):
        g = jnp.where(past, gate_t[h * nb:(h + 1) * nb, :], -jnp.inf)
        beaten = jnp.zeros((nb, blk), jnp.int32)
        for m in range(nb):
            gm = g[m:m + 1, :]
            beaten = beaten + ((gm > g) | ((gm == g) & (m < n_iota))).astype(jnp.int32)
        sel = past & (beaten < MOBA_TOPK)
        bias_ref[h * nb:(h + 1) * nb, :] = jnp.where(sel, 0.0, NEG_BIG)

    for h in range(N_HEADS):
        pair = q[:, (h // 2) * LANES:(h // 2 + 1) * LANES]
        qm_ref[h] = jnp.where(_head_lane_mask(h % 2, blk, LANES), pair, jnp.zeros_like(pair))

    def key_block(n, hp):
        return kb_ref[pl.ds(pl.multiple_of(n * blk, blk), blk), hp * LANES:(hp + 1) * LANES]

    key_i = lax.broadcasted_iota(jnp.int32, (blk, blk), 0)
    qry_i = lax.broadcasted_iota(jnp.int32, (blk, blk), 1)
    causal = key_i <= qry_i

    for h in range(N_HEADS):
        hp = h // 2
        s = _dot_nt(key_block(i, hp), qm_ref[h])
        s = jnp.where(causal, s, NEG_BIG)
        mx = jnp.max(s, axis=0, keepdims=True)
        p = jnp.exp(s - mx)
        m_ref[h:h + 1, :] = mx
        l_ref[h:h + 1, :] = jnp.sum(p, axis=0, keepdims=True)
        acc_ref[h * HEAD_DIM:(h + 1) * HEAD_DIM, :] = _dot(vt_ref[i, h * HEAD_DIM:(h + 1) * HEAD_DIM, :], p.astype(BF16))

    def past_block(n, carry):
        for h in range(N_HEADS):
            hp = h // 2
            s = _dot_nt(key_block(n, hp), qm_ref[h])
            s = s + bias_ref[pl.ds(h * nb + n, 1), :]
            m_old = m_ref[h:h + 1, :]
            m_new = jnp.maximum(m_old, jnp.max(s, axis=0, keepdims=True))
            alpha = jnp.exp(m_old - m_new)
            p = jnp.exp(s - m_new)
            m_ref[h:h + 1, :] = m_new
            l_ref[h:h + 1, :] = alpha * l_ref[h:h + 1, :] + jnp.sum(p, axis=0, keepdims=True)
            pv = _dot(vt_ref[n, h * HEAD_DIM:(h + 1) * HEAD_DIM, :], p.astype(BF16))
            acc_ref[h * HEAD_DIM:(h + 1) * HEAD_DIM, :] = alpha * acc_ref[h * HEAD_DIM:(h + 1) * HEAD_DIM, :] + pv
        return carry

    lax.fori_loop(0, i, past_block, 0)

    for h in range(N_HEADS):
        sl = slice(h * HEAD_DIM, (h + 1) * HEAD_DIM)
        acc_ref[sl, :] = acc_ref[sl, :] / l_ref[h:h + 1, :]
    att = acc_ref[...].T
    o_ref[...] = _mix_out(att, ga_ref[...], mc_ref[...], x_ref[...], mod_ref[2:3, :], ag_ref[...], wo_ref,
                          fg_ref[...], final)


def _moba_prompt(q, kb, vt, ga, mc, x, mod3, ag, wo_bf, fg, *, final):
    b, s, d = x.shape
    a = ATTN_WIDTH
    blk = MOBA_BLOCK
    assert s % blk == 0 and s // blk >= MOBA_TOPK
    nb = s // blk
    assert nb % 8 == 0
    tile = lambda w: pl.BlockSpec((None, blk, w), lambda bi, i: (bi, i, 0))
    seq = lambda r, w: pl.BlockSpec((None, r, w), lambda bi, i: (bi, 0, 0))
    const = lambda shape: pl.BlockSpec(shape, lambda bi, i: (0,) * len(shape))
    return pl.pallas_call(
        functools.partial(_moba_kernel, nb=nb, final=final),
        grid=(b, nb),
        in_specs=[tile(a), seq(s, a), seq(a, s), tile(a), tile(a), tile(d),
                  pl.BlockSpec((None, 3, d), lambda bi, i: (bi, 0, 0)),
                  const((1, a)), const((2 * a, d)), const((1, d))],
        out_specs=tile(d),
        out_shape=jax.ShapeDtypeStruct((b, s, d), F32),
        scratch_shapes=[
            pltpu.VMEM((nb, a, blk), BF16),
            pltpu.VMEM((N_HEADS * nb, a), F32),
            pltpu.VMEM((N_HEADS, blk, LANES), BF16),
            pltpu.VMEM((N_HEADS * nb, blk), F32),
            pltpu.VMEM((N_HEADS, blk), F32),
            pltpu.VMEM((N_HEADS, blk), F32),
            pltpu.VMEM((a, blk), F32),
        ],
        compiler_params=_params("arbitrary", "arbitrary"),
        name="moba",
    )(q, kb, vt, ga, mc, x, mod3, ag.reshape(1, a), wo_bf, fg.reshape(1, d))


def _paged_kernel(pt_ref, *refs, n_steps, page, t_new):
    del pt_ref
    pg = PAGES_PER_STEP
    kp = refs[:pg]
    vp = refs[pg:2 * pg]
    q_ref, kn_ref, vn_ref, o_ref, qbd_ref, sc_ref, so_ref, l_ref, acc_ref = refs[2 * pg:]
    a = ATTN_WIDTH
    rows = N_HEADS * t_new
    blk = MOBA_BLOCK
    n_keys = n_steps * pg * page
    nblk = n_keys // blk
    ppb = blk // page
    t = pl.program_id(1)

    @pl.when(t == 0)
    def _():
        q = q_ref[...].astype(F32)
        for h in range(N_HEADS):
            qbd_ref[h * t_new:(h + 1) * t_new, :] = jnp.where(_head_lane_mask(h, t_new, a), q, 0.0)

    @pl.when(t < n_steps)
    def _():
        qbd = qbd_ref[...].astype(BF16)
        for jp in range(pg):
            sc_ref[t * pg + jp] = _dot(qbd, kp[jp][...].astype(BF16))

    @pl.when(t == n_steps - 1)
    def _():
        lane_n = lax.broadcasted_iota(jnp.int32, (rows, LANES), 1)
        gate = jnp.full((rows, LANES), -jnp.inf, F32)
        for n in range(nblk):
            gsum = sum(jnp.sum(sc_ref[n * ppb + c], axis=1, keepdims=True) for c in range(ppb))
            gate = jnp.where(lane_n == n, gsum, gate)
        sel = jnp.zeros((rows, LANES), F32)
        for _ in range(MOBA_TOPK):
            mx = jnp.max(gate, axis=1, keepdims=True)
            first = jnp.min(jnp.where(gate == mx, lane_n, LANES), axis=1, keepdims=True)
            pick = lane_n == first
            sel = jnp.where(pick, 1.0, sel)
            gate = jnp.where(pick, -jnp.inf, gate)
        kn = jnp.concatenate([kn_ref[...], jnp.zeros((LANES - t_new, a), F32)], axis=0).astype(BF16)
        s_own = _dot_nt(qbd_ref[...].astype(BF16), kn)
        row_t = lax.broadcasted_iota(jnp.int32, (rows, LANES), 0) % t_new
        s_own = jnp.where(lane_n <= row_t, s_own, NEG_BIG)
        mx = jnp.max(s_own, axis=1, keepdims=True)
        for n in range(nblk):
            keep = jnp.max(jnp.where(lane_n == n, sel, 0.0), axis=1, keepdims=True) > 0.0
            for c in range(ppb):
                sb = jnp.where(keep, sc_ref[n * ppb + c], NEG_BIG)
                sc_ref[n * ppb + c] = sb
                mx = jnp.maximum(mx, jnp.max(sb, axis=1, keepdims=True))
        p_own = jnp.exp(s_own - mx)
        lsum = jnp.sum(p_own, axis=1, keepdims=True)
        for c in range(nblk * ppb):
            p = jnp.exp(sc_ref[c] - mx)
            sc_ref[c] = p
            lsum = lsum + jnp.sum(p, axis=1, keepdims=True)
        l_ref[...] = jnp.broadcast_to(lsum, (rows, LANES))
        so_ref[...] = p_own
        acc_ref[...] = jnp.zeros((rows, a), F32)

    @pl.when(t >= n_steps)
    def _():
        acc = acc_ref[...]
        for jp in range(pg):
            p = sc_ref[(t - n_steps) * pg + jp].astype(BF16)
            acc = acc + _dot_nt(p, vp[jp][...].astype(BF16))
        acc_ref[...] = acc

    @pl.when(t == 2 * n_steps - 1)
    def _():
        vn = jnp.concatenate([vn_ref[...], jnp.zeros((LANES - t_new, a), F32)], axis=0).astype(BF16)
        o = (acc_ref[...] + _dot(so_ref[...].astype(BF16), vn)) / l_ref[:, 0:1]
        att = jnp.zeros((t_new, a), F32)
        for h in range(N_HEADS):
            att = att + jnp.where(_head_lane_mask(h, t_new, a), o[h * t_new:(h + 1) * t_new, :], 0.0)
        o_ref[...] = att


def _paged_attention(q, kn, vn, cache_kt, cache_vt, page_table, layer):
    nseq, t_new, a = q.shape
    n_pages = page_table.shape[1]
    page = cache_kt.shape[3]
    pg = PAGES_PER_STEP
    assert n_pages % pg == 0 and MOBA_BLOCK % page == 0 and page == LANES and t_new == 8
    n_steps = n_pages // pg
    n_keys = n_pages * page
    assert n_keys // MOBA_BLOCK <= LANES and n_keys // MOBA_BLOCK >= MOBA_TOPK
    rows = N_HEADS * t_new

    def kspec(jp):
        return pl.BlockSpec((None, None, a, page),
                            lambda s, t, pt: (layer, pt[s, jnp.minimum(t, n_steps - 1) * pg + jp], 0, 0))

    def vspec(jp):
        return pl.BlockSpec((None, None, a, page),
                            lambda s, t, pt: (layer, pt[s, jnp.maximum(t - n_steps, 0) * pg + jp], 0, 0))

    seq = lambda: pl.BlockSpec((None, t_new, a), lambda s, t, pt: (s, 0, 0))
    grid_spec = pltpu.PrefetchScalarGridSpec(
        num_scalar_prefetch=1,
        grid=(nseq, 2 * n_steps),
        in_specs=[kspec(jp) for jp in range(pg)] + [vspec(jp) for jp in range(pg)] + [seq(), seq(), seq()],
        out_specs=seq(),
        scratch_shapes=[
            pltpu.VMEM((rows, a), F32),
            pltpu.VMEM((n_pages, rows, page), F32),
            pltpu.VMEM((rows, LANES), F32),
            pltpu.VMEM((rows, LANES), F32),
            pltpu.VMEM((rows, a), F32),
        ],
    )
    return pl.pallas_call(
        functools.partial(_paged_kernel, n_steps=n_steps, page=page, t_new=t_new),
        grid_spec=grid_spec,
        out_shape=jax.ShapeDtypeStruct((nseq, t_new, a), F32),
        compiler_params=_params("arbitrary", "arbitrary"),
        name="paged",
    )(page_table, *([cache_kt] * pg), *([cache_vt] * pg), q, kn, vn)


def _rope_tables(pos):
    half = HEAD_DIM // 2
    inv = ROPE_THETA ** (-jnp.arange(half, dtype=F32) * (2.0 / HEAD_DIM))
    ang = pos.astype(F32)[:, None] * inv[None, :]
    cos = jnp.cos(ang)
    sin = jnp.sin(ang)
    cos_t = jnp.tile(cos, (1, LANES // half))
    sin_t = jnp.tile(jnp.concatenate([-sin, sin], axis=1), (1, LANES // HEAD_DIM))
    return cos_t, sin_t, cos.T, sin.T


def kernel(x_prompt, x_sample, c_prompt, c_sample, cache_k, cache_v, state_conv, page_table, norm_gain, w_ada, b_ada,
           w_in, conv_dw_w, conv_dw_b, conv_ln_g, conv_ln_b, conv_pw2, attn_out_gain, conv_out_gain, w_out,
           final_gain):
    depth = w_in.shape[0]
    bp, sp, d = x_prompt.shape
    bs, ss, _ = x_sample.shape
    a = ATTN_WIDTH
    n_pool, page = cache_k.shape[1], cache_k.shape[2]
    past_len = page_table.shape[1] * page

    tab_p = _rope_tables(jnp.arange(sp, dtype=jnp.int32))
    tab_s = _rope_tables(past_len + jnp.arange(ss, dtype=jnp.int32))
    mod = _ada(jnp.concatenate([c_prompt, c_sample], axis=0), w_ada, b_ada)
    mod = mod.reshape(depth, bp + bs, 3, d)
    ckt = jnp.transpose(cache_k, (0, 1, 3, 4, 2)).reshape(depth, n_pool, a, page)
    cvt = jnp.transpose(cache_v, (0, 1, 3, 4, 2)).reshape(depth, n_pool, a, page)
    wi_bf = w_in.astype(BF16)
    wkv_t = jnp.transpose(w_in[:, :, a:3 * a], (0, 2, 1)).astype(BF16)
    pw2_bf = conv_pw2.astype(BF16)
    wo_bf = w_out.astype(BF16)
    zero_hist = jnp.zeros((bp, CONV_HIST, a), F32)

    hp, hs = x_prompt, x_sample
    kp_l, vp_l, cp_l, ks_l, vs_l, cs_l = [], [], [], [], [], []
    for l in range(depth):
        final = l == depth - 1
        conv_prm = (conv_dw_w[l], conv_dw_b[l], conv_ln_g[l], conv_ln_b[l], pw2_bf[l], conv_out_gain[l])
        mod_p, mod_s = mod[l, :bp], mod[l, bp:]

        q, kb, kt, vt, ga, mc, cst = _inproj(hp, mod_p, norm_gain[l], wi_bf[l], tab_p, zero_hist, *conv_prm, tm=256,
                                             wkv_t=wkv_t[l])
        hp = _moba_prompt(q, kb, vt, ga, mc, hp, mod_p, attn_out_gain[l], wo_bf[l], final_gain, final=final)
        kp_l.append(jnp.transpose(kt.reshape(bp, N_HEADS, HEAD_DIM, sp), (0, 3, 1, 2)))
        vp_l.append(jnp.transpose(vt.reshape(bp, N_HEADS, HEAD_DIM, sp), (0, 3, 1, 2)))
        cp_l.append(cst)

        q, k, v, ga, mc, cst = _inproj(hs, mod_s, norm_gain[l], wi_bf[l], tab_s, state_conv[l], *conv_prm, tm=ss)
        att = _paged_attention(q, k, v, ckt, cvt, page_table, l)
        hs = _outproj(att, ga, mc, hs, mod_s, attn_out_gain[l], wo_bf[l], final_gain, final=final)
        ks_l.append(k.reshape(bs, ss, N_HEADS, HEAD_DIM))
        vs_l.append(v.reshape(bs, ss, N_HEADS, HEAD_DIM))
        cs_l.append(cst)

    return (hp, hs, jnp.stack(kp_l), jnp.stack(vp_l), jnp.stack(cp_l),
            jnp.stack(ks_l), jnp.stack(vs_l), jnp.stack(cs_l))
```

```python
import functools
import math

import jax
import jax.numpy as jnp
from jax import lax
from jax.experimental import pallas as pl
from jax.experimental.pallas import tpu as pltpu

N_HEADS = 8
HEAD_DIM = 64
ATTN_WIDTH = N_HEADS * HEAD_DIM
MOBA_BLOCK = 256
MOBA_TOPK = 3
CONV_K = 31
CONV_HIST = CONV_K - 1
ROPE_THETA = 10000.0
NORM_EPS = 1e-6

LANES = 128
N_SLABS = ATTN_WIDTH // LANES
HIST_PAD = 32
NEG_BIG = -1e30
VMEM_LIMIT_BYTES = 56 * 1024 * 1024
PAGES_PER_STEP = 16
Q_SCALE = math.log2(math.e) / math.sqrt(HEAD_DIM)

F32 = jnp.float32
BF16 = jnp.bfloat16


def _silu(x):
    return x * jax.nn.sigmoid(x)


def _dot(a, b):
    return jnp.dot(a, b, preferred_element_type=F32)


def _dot_nt(a, b):
    return lax.dot_general(a, b, (((1,), (1,)), ((), ())), preferred_element_type=F32)


def _params(*semantics):
    return pltpu.CompilerParams(dimension_semantics=semantics, vmem_limit_bytes=VMEM_LIMIT_BYTES)


def _slab(c):
    return slice(c * LANES, (c + 1) * LANES)


def _ada_kernel(c_ref, wa_ref, ba_ref, o_ref):
    s = _silu(c_ref[...]).astype(BF16)
    o_ref[...] = _dot(s, wa_ref[...].astype(BF16)) + ba_ref[...]


def _ada(c_all, w_ada, b_ada):
    depth, d, n3 = w_ada.shape
    rows = c_all.shape[0]
    tn = 512
    return pl.pallas_call(
        _ada_kernel,
        grid=(depth, n3 // tn),
        in_specs=[
            pl.BlockSpec((rows, d), lambda l, n: (0, 0)),
            pl.BlockSpec((None, d, tn), lambda l, n: (l, 0, n)),
            pl.BlockSpec((None, 1, tn), lambda l, n: (l, 0, n)),
        ],
        out_specs=pl.BlockSpec((None, rows, tn), lambda l, n: (l, 0, n)),
        out_shape=jax.ShapeDtypeStruct((depth, rows, n3), F32),
        compiler_params=_params("arbitrary", "arbitrary"),
        name="ada",
    )(c_all, w_ada, b_ada.reshape(depth, 1, n3))


def _rope_tile(z, cos, sin_signed):
    lane = lax.broadcasted_iota(jnp.int32, (z.shape[0], LANES), 1)
    first_half = (lane % HEAD_DIM) < (HEAD_DIM // 2)
    outs = []
    for c in range(z.shape[1] // LANES):
        zc = z[:, _slab(c)]
        swapped = jnp.where(first_half,
                            pltpu.roll(zc, LANES - HEAD_DIM // 2, 1),
                            pltpu.roll(zc, HEAD_DIM // 2, 1))
        outs.append(zc * cos + swapped * sin_signed)
    return jnp.concatenate(outs, axis=1)


def _rope_tile_t(zt, cos_t, sin_t):
    half = HEAD_DIM // 2
    outs = []
    for h in range(N_HEADS):
        x1 = zt[h * HEAD_DIM:h * HEAD_DIM + half, :]
        x2 = zt[h * HEAD_DIM + half:(h + 1) * HEAD_DIM, :]
        outs.append(x1 * cos_t - x2 * sin_t)
        outs.append(x2 * cos_t + x1 * sin_t)
    return jnp.concatenate(outs, axis=0)


def _layernorm_swish(acc, lng, lnb):
    mu = jnp.mean(acc, axis=-1, keepdims=True)
    cen = acc - mu
    var = jnp.mean(cen * cen, axis=-1, keepdims=True)
    return _silu(cen * lax.rsqrt(var + NORM_EPS) * lng + lnb)


def _inproj_kernel(*refs, tm, rc, kv_t, layer):
    del layer
    (x_ref, mod_ref, ng_ref, wi_ref, cos_ref, sin_ref, hist_ref, dww_ref, dwb_ref, lng_ref, lnb_ref, pw2_ref,
     cg_ref) = refs[:13]
    refs = refs[13:]
    if kv_t:
        wkv_ref, cost_ref, sint_ref = refs[:3]
        refs = refs[3:]
        if len(refs) == 12:
            refs = refs[2:]
        q_ref, kt_ref, vt_ref, ga_ref, mc_ref, cs_ref, h_ref, xp_ref, gc_ref, sw_ref = refs
    else:
        q_ref, k_ref, v_ref, ga_ref, mc_ref, cs_ref, h_ref, xp_ref, gc_ref, sw_ref = refs
    a = ATTN_WIDTH
    off = HIST_PAD - CONV_HIST
    j = pl.program_id(1)

    @pl.when(j == 0)
    def _():
        hist = hist_ref[...]
        for c in range(N_SLABS):
            xp_ref[c, 0:off, :] = jnp.zeros((off, LANES), F32)
            xp_ref[c, off:HIST_PAD, :] = hist[:, _slab(c)]

    x = x_ref[...]
    shift = mod_ref[0:1, :]
    scale = mod_ref[1:2, :]
    xn = x * lax.rsqrt(jnp.mean(x * x, axis=-1, keepdims=True) + NORM_EPS)
    h_ref[...] = ((xn * ng_ref[...]) * (1.0 + scale) + shift).astype(BF16)

    def proj(g):
        return _dot(h_ref[...], wi_ref[:, g * a:(g + 1) * a])

    glu = proj(4) * jax.nn.sigmoid(proj(5))
    for c in range(N_SLABS):
        xp_ref[c, HIST_PAD:HIST_PAD + tm, :] = glu[:, _slab(c)]
    gc_ref[...] = _silu(proj(6))

    cos = cos_ref[...]
    sin = sin_ref[...]

    def emit_q():
        q_ref[...] = (_rope_tile(proj(0), cos, sin) * Q_SCALE).astype(BF16)

    def emit_ga():
        ga_ref[...] = _silu(proj(3))

    if kv_t:
        def emit_kt():
            kt_ref[...] = _rope_tile_t(_dot_nt(wkv_ref[0:a, :], h_ref[...]), cost_ref[...], sint_ref[...])

        def emit_vt():
            vt_ref[...] = _dot_nt(wkv_ref[a:2 * a, :], h_ref[...])

        matmul_tasks = [emit_q, emit_kt, emit_vt, emit_ga]
    else:
        def emit_k():
            k_ref[...] = _rope_tile(proj(1), cos, sin)

        def emit_v():
            v_ref[...] = proj(2)

        matmul_tasks = [emit_q, emit_k, emit_v, emit_ga]

    lng = lng_ref[...]
    lnb = lnb_ref[...]

    def conv_rows(read_window, write_rows):
        accs = []
        for c in range(N_SLABS):
            acc = jnp.zeros((rc, LANES), F32) + dwb_ref[:, _slab(c)]
            for tap in range(CONV_K):
                acc = acc + read_window(c, tap) * dww_ref[tap:tap + 1, _slab(c)]
            accs.append(acc)
        y = _layernorm_swish(jnp.concatenate(accs, axis=1), lng, lnb)
        for c in range(N_SLABS):
            write_rows(c, y[:, _slab(c)])

    def strided_chunk(r0, parity):
        def read_window(c, tap):
            return xp_ref[c, pl.ds(r0 + off + parity + tap, rc, stride=2), :]

        def write_rows(c, val):
            sw_ref[c, pl.ds(r0 + parity, rc, stride=2), :] = val

        return lambda: conv_rows(read_window, write_rows)

    def plain_chunk(r0):
        def read_window(c, tap):
            return xp_ref[c, r0 + off + tap:r0 + off + tap + rc, :]

        def write_rows(c, val):
            sw_ref[c, r0:r0 + rc, :] = val

        return lambda: conv_rows(read_window, write_rows)

    if kv_t:
        conv_tasks = [strided_chunk(r0, parity) for r0 in range(0, tm, 2 * rc) for parity in range(2)]
    else:
        conv_tasks = [plain_chunk(r0) for r0 in range(0, tm, rc)]

    n_conv, n_mm = len(conv_tasks), len(matmul_tasks)
    done_mm = 0
    for ci, conv_task in enumerate(conv_tasks):
        conv_task()
        want_mm = ((ci + 1) * n_mm) // n_conv
        while done_mm < want_mm:
            matmul_tasks[done_mm]()
            done_mm += 1

    sw = jnp.concatenate([sw_ref[c] for c in range(N_SLABS)], axis=1).astype(BF16)
    co = _dot(sw, pw2_ref[...])
    con = co * lax.rsqrt(jnp.mean(co * co, axis=-1, keepdims=True) + NORM_EPS) * cg_ref[...]
    mc_ref[...] = (con * gc_ref[...]).astype(BF16)

    cs_ref[...] = jnp.concatenate([xp_ref[c, HIST_PAD + tm - CONV_HIST:HIST_PAD + tm, :] for c in range(N_SLABS)],
                                  axis=1)
    for c in range(N_SLABS):
        tail = xp_ref[c, tm:tm + HIST_PAD, :]
        xp_ref[c, 0:HIST_PAD, :] = tail


def _inproj(x, mod3, ng, wi_bf, tables, hist, dww, dwb, lng, lnb, pw2_bf, cg, *, tm, wkv_t=None, layer=0, depth=1,
            kv_prev=None):
    b, s, d = x.shape
    a = ATTN_WIDTH
    n_in = wi_bf.shape[1]
    kv_t = wkv_t is not None
    rc = 64 if kv_t else min(tm, 32)
    assert s % tm == 0 and tm % rc == 0 and (not kv_t or tm % (2 * rc) == 0)
    cos_t, sin_t, cos_tt, sin_tt = tables
    row = lambda v: v.reshape(1, -1)
    const = lambda shape: pl.BlockSpec(shape, lambda bi, j: (0,) * len(shape))
    tile = lambda w: pl.BlockSpec((None, tm, w), lambda bi, j: (bi, j, 0))
    tile_t = pl.BlockSpec((None, None, a, tm), lambda bi, j: (layer, bi, 0, j))
    hist_spec = pl.BlockSpec((None, CONV_HIST, a), lambda bi, j: (bi, 0, 0))
    in_specs = [
        tile(d),
        pl.BlockSpec((None, 3, d), lambda bi, j: (bi, 0, 0)),
        const((1, d)),
        const((d, n_in)),
        pl.BlockSpec((tm, LANES), lambda bi, j: (j, 0)),
        pl.BlockSpec((tm, LANES), lambda bi, j: (j, 0)),
        hist_spec,
        const((CONV_K, a)),
        const((1, a)),
        const((1, a)),
        const((1, a)),
        const((a, a)),
        const((1, a)),
    ]
    args = [x, mod3, row(ng), wi_bf, cos_t, sin_t, hist, dww, row(dwb), row(lng), row(lnb), pw2_bf, row(cg)]
    q_shape = jax.ShapeDtypeStruct((b, s, a), BF16)
    tail_specs = [tile(a), tile(a), hist_spec]
    tail_shapes = [
        jax.ShapeDtypeStruct((b, s, a), F32),
        jax.ShapeDtypeStruct((b, s, a), BF16),
        jax.ShapeDtypeStruct((b, CONV_HIST, a), F32),
    ]
    aliases = {}
    if kv_t:
        half = HEAD_DIM // 2
        in_specs += [const((2 * a, d)),
                     pl.BlockSpec((half, tm), lambda bi, j: (0, j)),
                     pl.BlockSpec((half, tm), lambda bi, j: (0, j))]
        args += [wkv_t, cos_tt, sin_tt]
        if kv_prev is not None:
            aliases = {len(args): 1, len(args) + 1: 2}
            in_specs += [pl.BlockSpec(memory_space=pl.ANY)] * 2
            args += list(kv_prev)
        out_specs = [tile(a), tile_t, tile_t] + tail_specs
        out_shape = [q_shape,
                     jax.ShapeDtypeStruct((depth, b, a, s), F32),
                     jax.ShapeDtypeStruct((depth, b, a, s), F32),
                     ] + tail_shapes
    else:
        out_specs = [tile(a), tile(a), tile(a)] + tail_specs
        out_shape = [q_shape,
                     jax.ShapeDtypeStruct((b, s, a), F32),
                     jax.ShapeDtypeStruct((b, s, a), F32),
                     ] + tail_shapes
    return pl.pallas_call(
        functools.partial(_inproj_kernel, tm=tm, rc=rc, kv_t=kv_t, layer=layer),
        grid=(b, s // tm),
        in_specs=in_specs,
        out_specs=out_specs,
        out_shape=out_shape,
        input_output_aliases=aliases,
        scratch_shapes=[
            pltpu.VMEM((tm, d), BF16),
            pltpu.VMEM((N_SLABS, HIST_PAD + tm, LANES), F32),
            pltpu.VMEM((tm, a), F32),
            pltpu.VMEM((N_SLABS, tm, LANES), F32),
        ],
        compiler_params=_params("arbitrary", "arbitrary"),
        name="inproj_t" if kv_t else "inproj",
    )(*args)


def _mix_out(att, ga, mc_bf, x, gate, ag, wo_ref, fg, final):
    a = ATTN_WIDTH
    an = att * lax.rsqrt(jnp.mean(att * att, axis=-1, keepdims=True) + NORM_EPS) * ag
    ma = (an * ga).astype(BF16)
    y = _dot(ma, wo_ref[0:a, :]) + _dot(mc_bf, wo_ref[a:2 * a, :])
    xn = x + gate * y
    if final:
        xn = xn * lax.rsqrt(jnp.mean(xn * xn, axis=-1, keepdims=True) + NORM_EPS) * fg
    return xn


def _outproj_kernel(att_ref, ga_ref, mc_ref, x_ref, mod_ref, ag_ref, wo_ref, fg_ref, o_ref, *, final):
    o_ref[...] = _mix_out(att_ref[...], ga_ref[...], mc_ref[...], x_ref[...], mod_ref[2:3, :], ag_ref[...], wo_ref,
                          fg_ref[...], final)


def _outproj(att, ga, mc, x, mod3, ag, wo_bf, fg, *, final):
    b, s, d = x.shape
    a = ATTN_WIDTH
    tile = lambda w: pl.BlockSpec((None, s, w), lambda bi: (bi, 0, 0))
    const = lambda shape: pl.BlockSpec(shape, lambda bi: (0,) * len(shape))
    return pl.pallas_call(
        functools.partial(_outproj_kernel, final=final),
        grid=(b,),
        in_specs=[tile(a), tile(a), tile(a), tile(d), pl.BlockSpec((None, 3, d), lambda bi: (bi, 0, 0)),
                  const((1, a)), const((2 * a, d)), const((1, d))],
        out_specs=tile(d),
        out_shape=jax.ShapeDtypeStruct((b, s, d), F32),
        compiler_params=_params("arbitrary"),
        name="outproj",
    )(att, ga, mc, x, mod3, ag.reshape(1, a), wo_bf, fg.reshape(1, d))


def _head_lane_mask(h, rows, width):
    lane = lax.broadcasted_iota(jnp.int32, (rows, width), 1)
    return (lane // HEAD_DIM) == h


def _moba_kernel(q_ref, ktf_ref, vtf_ref, ga_ref, mc_ref, x_ref, mod_ref, ag_ref, wo_ref, fg_ref, o_ref,
                 kb_ref, vt_ref, km_ref, qm_ref, bias_ref, m_ref, l_ref, acc_ref, s0_ref, s1_ref, p_ref,
                 *, nb, final):
    blk = MOBA_BLOCK
    a = ATTN_WIDTH
    i = pl.program_id(1)

    @pl.when(i == 0)
    def _():
        for n in range(nb):
            vt_ref[n] = vtf_ref[:, n * blk:(n + 1) * blk].astype(BF16)
            kblk = ktf_ref[:, n * blk:(n + 1) * blk].T
            kb_ref[n * blk:(n + 1) * blk, :] = kblk.astype(BF16)
            mean_n = jnp.mean(kblk, axis=0, keepdims=True)
            for h in range(N_HEADS):
                km_ref[h * nb + n:h * nb + n + 1, :] = jnp.where(_head_lane_mask(h, 1, a), mean_n, 0.0)

    q = q_ref[...]
    for h in range(N_HEADS):
        pair = q[:, _slab(h // 2)]
        qm_ref[h] = jnp.where(_head_lane_mask(h % 2, blk, LANES), pair, jnp.zeros_like(pair))

    def key_block(n, hp):
        return kb_ref[pl.ds(pl.multiple_of(n * blk, blk), blk), _slab(hp)]

    def scores(n, dst_ref):
        for h in range(N_HEADS):
            dst_ref[h] = _dot_nt(key_block(n, h // 2), qm_ref[h])

    def values(n):
        return [_dot(vt_ref[n, h * HEAD_DIM:(h + 1) * HEAD_DIM, :], p_ref[h]) for h in range(N_HEADS)]

    scores(i, s0_ref)
    scores(0, s1_ref)

    gate_t = _dot_nt(km_ref[...].astype(BF16), q)
    n_iota = lax.broadcasted_iota(jnp.int32, (nb, blk), 0)
    past = n_iota < i
    for h in range(N_HEADS):
        g = jnp.where(past, gate_t[h * nb:(h + 1) * nb, :], -jnp.inf)
        beaten = jnp.zeros((nb, blk), jnp.int32)
        for m in range(nb):
            gm = g[m:m + 1, :]
            beaten = beaten + ((gm > g) | ((gm == g) & (m < n_iota))).astype(jnp.int32)
        sel = past & (beaten < MOBA_TOPK)
        bias_ref[h * nb:(h + 1) * nb, :] = jnp.where(sel, 0.0, NEG_BIG)

    key_i = lax.broadcasted_iota(jnp.int32, (blk, blk), 0)
    qry_i = lax.broadcasted_iota(jnp.int32, (blk, blk), 1)
    causal = key_i <= qry_i
    for h in range(N_HEADS):
        s = jnp.where(causal, s0_ref[h], NEG_BIG)
        mx = jnp.max(s, axis=0, keepdims=True)
        p = jnp.exp2(s - mx)
        m_ref[h:h + 1, :] = mx
        l_ref[h:h + 1, :] = jnp.sum(p, axis=0, keepdims=True)
        p_ref[h] = p.astype(BF16)
    for h, pv in enumerate(values(i)):
        acc_ref[h * HEAD_DIM:(h + 1) * HEAD_DIM, :] = pv

    def past_step(n, cur_ref, nxt_ref):
        scores(n + 1, nxt_ref)
        alphas = []
        for h in range(N_HEADS):
            s = cur_ref[h]
            b = bias_ref[pl.ds(h * nb + n, 1), :]
            m_old = m_ref[h:h + 1, :]
            m_new = jnp.maximum(m_old, jnp.max(s, axis=0, keepdims=True) + b)
            alpha = jnp.exp2(m_old - m_new)
            p = jnp.exp2(s - (m_new - b))
            m_ref[h:h + 1, :] = m_new
            l_ref[h:h + 1, :] = alpha * l_ref[h:h + 1, :] + jnp.sum(p, axis=0, keepdims=True)
            p_ref[h] = p.astype(BF16)
            alphas.append(alpha)
        for h, pv in enumerate(values(n)):
            sl = slice(h * HEAD_DIM, (h + 1) * HEAD_DIM)
            acc_ref[sl, :] = alphas[h] * acc_ref[sl, :] + pv

    def past_block(n, carry):
        @pl.when(n % 2 == 0)
        def _():
            past_step(n, s1_ref, s0_ref)

        @pl.when(n % 2 == 1)
        def _():
            past_step(n, s0_ref, s1_ref)

        return carry

    lax.fori_loop(0, i, past_block, 0)

    for h in range(N_HEADS):
        sl = slice(h * HEAD_DIM, (h + 1) * HEAD_DIM)
        acc_ref[sl, :] = acc_ref[sl, :] / l_ref[h:h + 1, :]
    att = acc_ref[...].T
    o_ref[...] = _mix_out(att, ga_ref[...], mc_ref[...], x_ref[...], mod_ref[2:3, :], ag_ref[...], wo_ref,
                          fg_ref[...], final)


def _moba_prompt(q, kt_all, vt_all, layer, ga, mc, x, mod3, ag, wo_bf, fg, *, final):
    b, s, d = x.shape
    a = ATTN_WIDTH
    blk = MOBA_BLOCK
    assert s % blk == 0 and s // blk >= MOBA_TOPK
    nb = s // blk
    assert nb % 8 == 0
    tile = lambda w: pl.BlockSpec((None, blk, w), lambda bi, i: (bi, i, 0))
    const = lambda shape: pl.BlockSpec(shape, lambda bi, i: (0,) * len(shape))
    return pl.pallas_call(
        functools.partial(_moba_kernel, nb=nb, final=final),
        grid=(b, nb),
        in_specs=[tile(a),
                  pl.BlockSpec((None, None, a, s), lambda bi, i: (layer, bi, 0, 0)),
                  pl.BlockSpec((None, None, a, s), lambda bi, i: (layer, bi, 0, 0)),
                  tile(a), tile(a), tile(d),
                  pl.BlockSpec((None, 3, d), lambda bi, i: (bi, 0, 0)),
                  const((1, a)), const((2 * a, d)), const((1, d))],
        out_specs=tile(d),
        out_shape=jax.ShapeDtypeStruct((b, s, d), F32),
        scratch_shapes=[
            pltpu.VMEM((s, a), BF16),
            pltpu.VMEM((nb, a, blk), BF16),
            pltpu.VMEM((N_HEADS * nb, a), F32),
            pltpu.VMEM((N_HEADS, blk, LANES), BF16),
            pltpu.VMEM((N_HEADS * nb, blk), F32),
            pltpu.VMEM((N_HEADS, blk), F32),
            pltpu.VMEM((N_HEADS, blk), F32),
            pltpu.VMEM((a, blk), F32),
            pltpu.VMEM((N_HEADS, blk, blk), F32),
            pltpu.VMEM((N_HEADS, blk, blk), F32),
            pltpu.VMEM((N_HEADS, blk, blk), BF16),
        ],
        compiler_params=_params("arbitrary", "arbitrary"),
        name="moba",
    )(q, kt_all, vt_all, ga, mc, x, mod3, ag.reshape(1, a), wo_bf, fg.reshape(1, d))


def _paged_kernel(pt_ref, *refs, n_steps, page, t_new):
    del pt_ref
    pg = PAGES_PER_STEP
    kp = refs[:pg]
    vp = refs[pg:2 * pg]
    (q_ref, kn_ref, vn_ref, o_ref,
     qbd_ref, sc_ref, gate_ref, bmax_ref, idx_ref, m_ref, so_ref, l_ref, acc_ref) = refs[2 * pg:]
    a = ATTN_WIDTH
    rows = N_HEADS * t_new
    blk = MOBA_BLOCK
    ppb = blk // page
    bps = pg // ppb
    t = pl.program_id(1)
    lane_n = lax.broadcasted_iota(jnp.int32, (rows, LANES), 1)

    @pl.when(t == 0)
    def _():
        q = q_ref[...].astype(F32)
        for h in range(N_HEADS):
            qbd_ref[h * t_new:(h + 1) * t_new, :] = jnp.where(_head_lane_mask(h, t_new, a), q, 0.0)
        gate_ref[...] = jnp.full((rows, LANES), -jnp.inf, F32)
        bmax_ref[...] = jnp.full((rows, LANES), -jnp.inf, F32)

    @pl.when(t < n_steps)
    def _():
        qbd = qbd_ref[...].astype(BF16)
        gate = gate_ref[...]
        bmax = bmax_ref[...]
        for bi in range(bps):
            tot = None
            top = None
            for c in range(ppb):
                jp = bi * ppb + c
                s = _dot(qbd, kp[jp][...].astype(BF16))
                sc_ref[t * pg + jp] = s
                tot = s if tot is None else tot + s
                top = s if top is None else jnp.maximum(top, s)
            n = t * bps + bi
            gate = jnp.where(lane_n == n, jnp.sum(tot, axis=1, keepdims=True), gate)
            bmax = jnp.where(lane_n == n, jnp.max(top, axis=1, keepdims=True), bmax)
        gate_ref[...] = gate
        bmax_ref[...] = bmax

    @pl.when(t == n_steps - 1)
    def _():
        gate = gate_ref[...]
        sel = jnp.zeros((rows, LANES), jnp.bool_)
        for r in range(MOBA_TOPK):
            mx = jnp.max(gate, axis=1, keepdims=True)
            first = jnp.min(jnp.where(gate == mx, lane_n, LANES), axis=1, keepdims=True)
            pick = lane_n == first
            idx_ref[r] = jnp.broadcast_to(first, (rows, LANES))
            sel = sel | pick
            gate = jnp.where(pick, -jnp.inf, gate)
        kn = jnp.concatenate([kn_ref[...], jnp.zeros((LANES - t_new, a), F32)], axis=0).astype(BF16)
        s_own = _dot_nt(qbd_ref[...].astype(BF16), kn)
        row_t = lax.broadcasted_iota(jnp.int32, (rows, LANES), 0) % t_new
        s_own = jnp.where(lane_n <= row_t, s_own, NEG_BIG)
        mx = jnp.maximum(jnp.max(s_own, axis=1, keepdims=True),
                         jnp.max(jnp.where(sel, bmax_ref[...], NEG_BIG), axis=1, keepdims=True))
        p_own = jnp.exp2(s_own - mx)
        m_ref[...] = jnp.broadcast_to(mx, (rows, LANES))
        so_ref[...] = p_own
        l_ref[...] = p_own
        acc_ref[...] = jnp.zeros((rows, a), F32)

    @pl.when(t >= n_steps)
    def _():
        c0 = t - n_steps
        mx = m_ref[...]
        picks = [idx_ref[r] for r in range(MOBA_TOPK)]
        acc = acc_ref[...]
        lpart = l_ref[...]
        for jp in range(pg):
            n = c0 * bps + jp // ppb
            keep = picks[0] == n
            for r in range(1, MOBA_TOPK):
                keep = keep | (picks[r] == n)
            p = jnp.where(keep, jnp.exp2(sc_ref[c0 * pg + jp] - mx), 0.0)
            lpart = lpart + p
            acc = acc + _dot_nt(p.astype(BF16), vp[jp][...].astype(BF16))
        acc_ref[...] = acc
        l_ref[...] = lpart

    @pl.when(t == 2 * n_steps - 1)
    def _():
        vn = jnp.concatenate([vn_ref[...], jnp.zeros((LANES - t_new, a), F32)], axis=0).astype(BF16)
        lsum = jnp.sum(l_ref[...], axis=1, keepdims=True)
        o = (acc_ref[...] + _dot(so_ref[...].astype(BF16), vn)) / lsum
        att = jnp.zeros((t_new, a), F32)
        for h in range(N_HEADS):
            att = att + jnp.where(_head_lane_mask(h, t_new, a), o[h * t_new:(h + 1) * t_new, :], 0.0)
        o_ref[...] = att


def _paged_attention(q, kn, vn, cache_kt, cache_vt, page_table, layer):
    nseq, t_new, a = q.shape
    n_pages = page_table.shape[1]
    page = cache_kt.shape[3]
    pg = PAGES_PER_STEP
    assert n_pages % pg == 0 and MOBA_BLOCK % page == 0 and page == LANES and t_new == 8
    assert pg % (MOBA_BLOCK // page) == 0
    n_steps = n_pages // pg
    n_keys = n_pages * page
    assert n_keys // MOBA_BLOCK <= LANES and n_keys // MOBA_BLOCK >= MOBA_TOPK
    rows = N_HEADS * t_new

    def kspec(jp):
        return pl.BlockSpec((None, None, a, page),
                            lambda s, t, pt: (layer, pt[s, jnp.minimum(t, n_steps - 1) * pg + jp], 0, 0))

    def vspec(jp):
        return pl.BlockSpec((None, None, a, page),
                            lambda s, t, pt: (layer, pt[s, jnp.maximum(t - n_steps, 0) * pg + jp], 0, 0))

    seq = lambda: pl.BlockSpec((None, t_new, a), lambda s, t, pt: (s, 0, 0))
    grid_spec = pltpu.PrefetchScalarGridSpec(
        num_scalar_prefetch=1,
        grid=(nseq, 2 * n_steps),
        in_specs=[kspec(jp) for jp in range(pg)] + [vspec(jp) for jp in range(pg)] + [seq(), seq(), seq()],
        out_specs=seq(),
        scratch_shapes=[
            pltpu.VMEM((rows, a), F32),
            pltpu.VMEM((n_pages, rows, page), F32),
            pltpu.VMEM((rows, LANES), F32),
            pltpu.VMEM((rows, LANES), F32),
            pltpu.VMEM((MOBA_TOPK, rows, LANES), jnp.int32),
            pltpu.VMEM((rows, LANES), F32),
            pltpu.VMEM((rows, LANES), F32),
            pltpu.VMEM((rows, LANES), F32),
            pltpu.VMEM((rows, a), F32),
        ],
    )
    return pl.pallas_call(
        functools.partial(_paged_kernel, n_steps=n_steps, page=page, t_new=t_new),
        grid_spec=grid_spec,
        out_shape=jax.ShapeDtypeStruct((nseq, t_new, a), F32),
        compiler_params=_params("arbitrary", "arbitrary"),
        name="paged",
    )(page_table, *([cache_kt] * pg), *([cache_vt] * pg), q, kn, vn)


def _rope_tables(pos):
    half = HEAD_DIM // 2
    inv = ROPE_THETA ** (-jnp.arange(half, dtype=F32) * (2.0 / HEAD_DIM))
    ang = pos.astype(F32)[:, None] * inv[None, :]
    cos = jnp.cos(ang)
    sin = jnp.sin(ang)
    cos_t = jnp.tile(cos, (1, LANES // half))
    sin_t = jnp.tile(jnp.concatenate([-sin, sin], axis=1), (1, LANES // HEAD_DIM))
    return cos_t, sin_t, cos.T, sin.T


def _heads_last(xt, bp, sp):
    depth = xt.shape[0]
    return jnp.transpose(xt.reshape(depth, bp, N_HEADS, HEAD_DIM, sp), (0, 1, 4, 2, 3))


def kernel(x_prompt, x_sample, c_prompt, c_sample, cache_k, cache_v, state_conv, page_table, norm_gain, w_ada, b_ada,
           w_in, conv_dw_w, conv_dw_b, conv_ln_g, conv_ln_b, conv_pw2, attn_out_gain, conv_out_gain, w_out,
           final_gain):
    depth = w_in.shape[0]
    bp, sp, d = x_prompt.shape
    bs, ss, _ = x_sample.shape
    a = ATTN_WIDTH
    n_pool, page = cache_k.shape[1], cache_k.shape[2]
    past_len = page_table.shape[1] * page

    tab_p = _rope_tables(jnp.arange(sp, dtype=jnp.int32))
    tab_s = _rope_tables(past_len + jnp.arange(ss, dtype=jnp.int32))
    mod = _ada(jnp.concatenate([c_prompt, c_sample], axis=0), w_ada, b_ada)
    mod = mod.reshape(depth, bp + bs, 3, d)
    ckt = jnp.transpose(cache_k, (0, 1, 3, 4, 2)).reshape(depth, n_pool, a, page)
    cvt = jnp.transpose(cache_v, (0, 1, 3, 4, 2)).reshape(depth, n_pool, a, page)
    wi_bf = w_in.astype(BF16)
    wkv_t = jnp.transpose(w_in[:, :, a:3 * a], (0, 2, 1)).astype(BF16)
    pw2_bf = conv_pw2.astype(BF16)
    wo_bf = w_out.astype(BF16)
    zero_hist = jnp.zeros((bp, CONV_HIST, a), F32)

    hp, hs = x_prompt, x_sample
    kv_prev = None
    cp_l, ks_l, vs_l, cs_l = [], [], [], []
    for l in range(depth):
        final = l == depth - 1
        conv_prm = (conv_dw_w[l], conv_dw_b[l], conv_ln_g[l], conv_ln_b[l], pw2_bf[l], conv_out_gain[l])
        mod_p, mod_s = mod[l, :bp], mod[l, bp:]

        q, kt_all, vt_all, ga, mc, cst = _inproj(hp, mod_p, norm_gain[l], wi_bf[l], tab_p, zero_hist, *conv_prm,
                                                 tm=256, wkv_t=wkv_t[l], layer=l, depth=depth, kv_prev=kv_prev)
        kv_prev = (kt_all, vt_all)
        hp = _moba_prompt(q, kt_all, vt_all, l, ga, mc, hp, mod_p, attn_out_gain[l], wo_bf[l], final_gain,
                          final=final)
        cp_l.append(cst)

        q, k, v, ga, mc, cst = _inproj(hs, mod_s, norm_gain[l], wi_bf[l], tab_s, state_conv[l], *conv_prm, tm=ss)
        att = _paged_attention(q, k, v, ckt, cvt, page_table, l)
        hs = _outproj(att, ga, mc, hs, mod_s, attn_out_gain[l], wo_bf[l], final_gain, final=final)
        ks_l.append(k.reshape(bs, ss, N_HEADS, HEAD_DIM))
        vs_l.append(v.reshape(bs, ss, N_HEADS, HEAD_DIM))
        cs_l.append(cst)

    kt_all, vt_all = kv_prev
    return (hp, hs, _heads_last(kt_all, bp, sp), _heads_last(vt_all, bp, sp), jnp.stack(cp_l),
            jnp.stack(ks_l), jnp.stack(vs_l), jnp.stack(cs_l))
```

```python
import functools
import math

import jax
import jax.numpy as jnp
from jax import lax
from jax.experimental import pallas as pl
from jax.experimental.pallas import tpu as pltpu

N_HEADS = 8
HEAD_DIM = 64
ATTN_WIDTH = N_HEADS * HEAD_DIM
MOBA_BLOCK = 256
MOBA_TOPK = 3
CONV_K = 31
CONV_HIST = CONV_K - 1
ROPE_THETA = 10000.0
NORM_EPS = 1e-6

LANES = 128
N_SLABS = ATTN_WIDTH // LANES
HIST_PAD = 32
NEG_BIG = -1e30
VMEM_LIMIT_BYTES = 56 * 1024 * 1024
PAGES_PER_STEP = 16
Q_SCALE = math.log2(math.e) / math.sqrt(HEAD_DIM)

F32 = jnp.float32
BF16 = jnp.bfloat16


def _silu(x):
    return x * jax.nn.sigmoid(x)


def _dot(a, b):
    return jnp.dot(a, b, preferred_element_type=F32)


def _dot_nt(a, b):
    return lax.dot_general(a, b, (((1,), (1,)), ((), ())), preferred_element_type=F32)


def _params(*semantics, flags=None):
    return pltpu.CompilerParams(dimension_semantics=semantics, vmem_limit_bytes=VMEM_LIMIT_BYTES, flags=flags)


def _slab(c):
    return slice(c * LANES, (c + 1) * LANES)


def _ada_kernel(c_ref, wa_ref, ba_ref, o_ref):
    s = _silu(c_ref[...]).astype(BF16)
    o_ref[...] = _dot(s, wa_ref[...].astype(BF16)) + ba_ref[...]


def _ada(c_all, w_ada, b_ada):
    depth, d, n3 = w_ada.shape
    rows = c_all.shape[0]
    tn = 512
    return pl.pallas_call(
        _ada_kernel,
        grid=(depth, n3 // tn),
        in_specs=[
            pl.BlockSpec((rows, d), lambda l, n: (0, 0)),
            pl.BlockSpec((None, d, tn), lambda l, n: (l, 0, n)),
            pl.BlockSpec((None, 1, tn), lambda l, n: (l, 0, n)),
        ],
        out_specs=pl.BlockSpec((None, rows, tn), lambda l, n: (l, 0, n)),
        out_shape=jax.ShapeDtypeStruct((depth, rows, n3), F32),
        compiler_params=_params("arbitrary", "arbitrary"),
        name="ada",
    )(c_all, w_ada, b_ada.reshape(depth, 1, n3))


def _rope_tile(z, cos, sin_signed):
    lane = lax.broadcasted_iota(jnp.int32, (z.shape[0], LANES), 1)
    first_half = (lane % HEAD_DIM) < (HEAD_DIM // 2)
    outs = []
    for c in range(z.shape[1] // LANES):
        zc = z[:, _slab(c)]
        swapped = jnp.where(first_half,
                            pltpu.roll(zc, LANES - HEAD_DIM // 2, 1),
                            pltpu.roll(zc, HEAD_DIM // 2, 1))
        outs.append(zc * cos + swapped * sin_signed)
    return jnp.concatenate(outs, axis=1)


def _rope_tile_t(zt, cos_t, sin_t):
    half = HEAD_DIM // 2
    outs = []
    for h in range(N_HEADS):
        x1 = zt[h * HEAD_DIM:h * HEAD_DIM + half, :]
        x2 = zt[h * HEAD_DIM + half:(h + 1) * HEAD_DIM, :]
        outs.append(x1 * cos_t - x2 * sin_t)
        outs.append(x2 * cos_t + x1 * sin_t)
    return jnp.concatenate(outs, axis=0)


def _layernorm_swish(acc, lng, lnb):
    mu = jnp.mean(acc, axis=-1, keepdims=True)
    cen = acc - mu
    var = jnp.mean(cen * cen, axis=-1, keepdims=True)
    return _silu(cen * lax.rsqrt(var + NORM_EPS) * lng + lnb)


def _inproj_kernel(*refs, tm, rc, kv_t, fill_layers):
    (x_ref, mod_ref, ng_ref, wi_ref, cos_ref, sin_ref, hist_ref, dww_ref, dwb_ref, lng_ref, lnb_ref, pw2_ref,
     cg_ref) = refs[:13]
    refs = refs[13:]
    if kv_t:
        wkv_ref, cost_ref, sint_ref = refs[:3]
        refs = refs[3:]
        if len(refs) == 12:
            refs = refs[2:]
        q_ref, kt_ref, vt_ref, ga_ref, mc_ref, cs_ref, h_ref, xp_ref, gc_ref, sw_ref = refs
    else:
        q_ref, k_ref, v_ref, ga_ref, mc_ref, cs_ref, h_ref, xp_ref, gc_ref, sw_ref = refs
    a = ATTN_WIDTH
    off = HIST_PAD - CONV_HIST
    j = pl.program_id(1)

    @pl.when(j == 0)
    def _():
        hist = hist_ref[...]
        for c in range(N_SLABS):
            xp_ref[c, 0:off, :] = jnp.zeros((off, LANES), F32)
            xp_ref[c, off:HIST_PAD, :] = hist[:, _slab(c)]

    x = x_ref[...]
    shift = mod_ref[0:1, :]
    scale = mod_ref[1:2, :]
    xn = x * lax.rsqrt(jnp.mean(x * x, axis=-1, keepdims=True) + NORM_EPS)
    h_ref[...] = ((xn * ng_ref[...]) * (1.0 + scale) + shift).astype(BF16)

    def proj(g):
        return _dot(h_ref[...], wi_ref[:, g * a:(g + 1) * a])

    glu = proj(4) * jax.nn.sigmoid(proj(5))
    for c in range(N_SLABS):
        xp_ref[c, HIST_PAD:HIST_PAD + tm, :] = glu[:, _slab(c)]
    gc_ref[...] = _silu(proj(6))

    cos = cos_ref[...]
    sin = sin_ref[...]

    def emit_q():
        q_ref[...] = (_rope_tile(proj(0), cos, sin) * Q_SCALE).astype(BF16)

    def emit_ga():
        ga_ref[...] = _silu(proj(3))

    if kv_t:
        def put_t(dst_ref, val):
            if fill_layers is None:
                dst_ref[...] = val
            else:
                own, depth = fill_layers
                for o in range(depth):
                    dst_ref[o] = val if o == own else jnp.zeros_like(val)

        def emit_kt():
            put_t(kt_ref, _rope_tile_t(_dot_nt(wkv_ref[0:a, :], h_ref[...]), cost_ref[...], sint_ref[...]))

        def emit_vt():
            put_t(vt_ref, _dot_nt(wkv_ref[a:2 * a, :], h_ref[...]))

        matmul_tasks = [emit_q, emit_kt, emit_vt, emit_ga]
    else:
        def emit_k():
            k_ref[...] = _rope_tile(proj(1), cos, sin)

        def emit_v():
            v_ref[...] = proj(2)

        matmul_tasks = [emit_q, emit_k, emit_v, emit_ga]

    def conv_slab(c):
        bias = dwb_ref[:, _slab(c)]
        if kv_t:
            half = tm // 2
            for parity in range(2):
                acc = jnp.zeros((half, LANES), F32) + bias
                for tap in range(CONV_K):
                    win = xp_ref[c, pl.ds(off + parity + tap, half, stride=2), :]
                    acc = acc + win * dww_ref[tap:tap + 1, _slab(c)]
                sw_ref[c, pl.ds(parity, half, stride=2), :] = acc
        else:
            acc = jnp.zeros((tm, LANES), F32) + bias
            for tap in range(CONV_K):
                acc = acc + xp_ref[c, off + tap:off + tap + tm, :] * dww_ref[tap:tap + 1, _slab(c)]
            sw_ref[c, :, :] = acc

    conv_tasks = [functools.partial(conv_slab, c) for c in range(N_SLABS)]

    for task in range(max(len(conv_tasks), len(matmul_tasks))):
        if task < len(conv_tasks):
            conv_tasks[task]()
        if task < len(matmul_tasks):
            matmul_tasks[task]()

    lng = lng_ref[...]
    lnb = lnb_ref[...]
    for r0 in range(0, tm, rc):
        dw = jnp.concatenate([sw_ref[c, r0:r0 + rc, :] for c in range(N_SLABS)], axis=1)
        y = _layernorm_swish(dw, lng, lnb)
        for c in range(N_SLABS):
            sw_ref[c, r0:r0 + rc, :] = y[:, _slab(c)]

    sw = jnp.concatenate([sw_ref[c] for c in range(N_SLABS)], axis=1).astype(BF16)
    co = _dot(sw, pw2_ref[...])
    con = co * lax.rsqrt(jnp.mean(co * co, axis=-1, keepdims=True) + NORM_EPS) * cg_ref[...]
    mc_ref[...] = (con * gc_ref[...]).astype(BF16)

    cs_ref[...] = jnp.concatenate([xp_ref[c, HIST_PAD + tm - CONV_HIST:HIST_PAD + tm, :] for c in range(N_SLABS)],
                                  axis=1)
    for c in range(N_SLABS):
        tail = xp_ref[c, tm:tm + HIST_PAD, :]
        xp_ref[c, 0:HIST_PAD, :] = tail


def _inproj(x, mod3, ng, wi_bf, tables, hist, dww, dwb, lng, lnb, pw2_bf, cg, *, tm, wkv_t=None, layer=0, depth=1,
            kv_prev=None):
    b, s, d = x.shape
    a = ATTN_WIDTH
    n_in = wi_bf.shape[1]
    kv_t = wkv_t is not None
    rc = 64 if kv_t else min(tm, 32)
    assert s % tm == 0 and tm % rc == 0 and (not kv_t or tm % (2 * rc) == 0)
    cos_t, sin_t, cos_tt, sin_tt = tables
    row = lambda v: v.reshape(1, -1)
    const = lambda shape: pl.BlockSpec(shape, lambda bi, j: (0,) * len(shape))
    tile = lambda w: pl.BlockSpec((None, tm, w), lambda bi, j: (bi, j, 0))
    if kv_prev is None:
        tile_t = pl.BlockSpec((depth, None, a, tm), lambda bi, j: (0, bi, 0, j))
        fill_layers = (layer, depth)
    else:
        tile_t = pl.BlockSpec((None, None, a, tm), lambda bi, j: (layer, bi, 0, j))
        fill_layers = None
    hist_spec = pl.BlockSpec((None, CONV_HIST, a), lambda bi, j: (bi, 0, 0))
    in_specs = [
        tile(d),
        pl.BlockSpec((None, 3, d), lambda bi, j: (bi, 0, 0)),
        const((1, d)),
        const((d, n_in)),
        pl.BlockSpec((tm, LANES), lambda bi, j: (j, 0)),
        pl.BlockSpec((tm, LANES), lambda bi, j: (j, 0)),
        hist_spec,
        const((CONV_K, a)),
        const((1, a)),
        const((1, a)),
        const((1, a)),
        const((a, a)),
        const((1, a)),
    ]
    args = [x, mod3, row(ng), wi_bf, cos_t, sin_t, hist, dww, row(dwb), row(lng), row(lnb), pw2_bf, row(cg)]
    q_shape = jax.ShapeDtypeStruct((b, s, a), BF16)
    tail_specs = [tile(a), tile(a), hist_spec]
    tail_shapes = [
        jax.ShapeDtypeStruct((b, s, a), F32),
        jax.ShapeDtypeStruct((b, s, a), BF16),
        jax.ShapeDtypeStruct((b, CONV_HIST, a), F32),
    ]
    aliases = {}
    if kv_t:
        half = HEAD_DIM // 2
        in_specs += [const((2 * a, d)),
                     pl.BlockSpec((half, tm), lambda bi, j: (0, j)),
                     pl.BlockSpec((half, tm), lambda bi, j: (0, j))]
        args += [wkv_t, cos_tt, sin_tt]
        if kv_prev is not None:
            aliases = {len(args): 1, len(args) + 1: 2}
            in_specs += [pl.BlockSpec(memory_space=pl.ANY)] * 2
            args += list(kv_prev)
        out_specs = [tile(a), tile_t, tile_t] + tail_specs
        out_shape = [q_shape,
                     jax.ShapeDtypeStruct((depth, b, a, s), F32),
                     jax.ShapeDtypeStruct((depth, b, a, s), F32),
                     ] + tail_shapes
    else:
        out_specs = [tile(a), tile(a), tile(a)] + tail_specs
        out_shape = [q_shape,
                     jax.ShapeDtypeStruct((b, s, a), F32),
                     jax.ShapeDtypeStruct((b, s, a), F32),
                     ] + tail_shapes
    return pl.pallas_call(
        functools.partial(_inproj_kernel, tm=tm, rc=rc, kv_t=kv_t, fill_layers=fill_layers if kv_t else None),
        grid=(b, s // tm),
        in_specs=in_specs,
        out_specs=out_specs,
        out_shape=out_shape,
        input_output_aliases=aliases,
        scratch_shapes=[
            pltpu.VMEM((tm, d), BF16),
            pltpu.VMEM((N_SLABS, HIST_PAD + tm, LANES), F32),
            pltpu.VMEM((tm, a), F32),
            pltpu.VMEM((N_SLABS, tm, LANES), F32),
        ],
        compiler_params=_params("arbitrary", "arbitrary"),
        name="inproj_t" if kv_t else "inproj",
    )(*args)


def _mix_out(att, ga, mc_bf, x, gate, ag, wo_ref, fg, final):
    a = ATTN_WIDTH
    an = att * lax.rsqrt(jnp.mean(att * att, axis=-1, keepdims=True) + NORM_EPS) * ag
    ma = (an * ga).astype(BF16)
    y = _dot(ma, wo_ref[0:a, :]) + _dot(mc_bf, wo_ref[a:2 * a, :])
    xn = x + gate * y
    if final:
        xn = xn * lax.rsqrt(jnp.mean(xn * xn, axis=-1, keepdims=True) + NORM_EPS) * fg
    return xn


def _outproj_kernel(att_ref, ga_ref, mc_ref, x_ref, mod_ref, ag_ref, wo_ref, fg_ref, o_ref, *, final):
    o_ref[...] = _mix_out(att_ref[...], ga_ref[...], mc_ref[...], x_ref[...], mod_ref[2:3, :], ag_ref[...], wo_ref,
                          fg_ref[...], final)


def _outproj(att, ga, mc, x, mod3, ag, wo_bf, fg, *, final):
    b, s, d = x.shape
    a = ATTN_WIDTH
    tile = lambda w: pl.BlockSpec((None, s, w), lambda bi: (bi, 0, 0))
    const = lambda shape: pl.BlockSpec(shape, lambda bi: (0,) * len(shape))
    return pl.pallas_call(
        functools.partial(_outproj_kernel, final=final),
        grid=(b,),
        in_specs=[tile(a), tile(a), tile(a), tile(d), pl.BlockSpec((None, 3, d), lambda bi: (bi, 0, 0)),
                  const((1, a)), const((2 * a, d)), const((1, d))],
        out_specs=tile(d),
        out_shape=jax.ShapeDtypeStruct((b, s, d), F32),
        compiler_params=_params("arbitrary"),
        name="outproj",
    )(att, ga, mc, x, mod3, ag.reshape(1, a), wo_bf, fg.reshape(1, d))


def _head_lane_mask(h, rows, width):
    lane = lax.broadcasted_iota(jnp.int32, (rows, width), 1)
    return (lane // HEAD_DIM) == h


def _moba_kernel(q_ref, ktf_ref, vtf_ref, ga_ref, mc_ref, x_ref, mod_ref, ag_ref, wo_ref, fg_ref, o_ref,
                 kb_ref, vt_ref, km_ref, qm_ref, bias_ref, m_ref, l_ref, acc_ref, s0_ref, s1_ref, p_ref,
                 *, nb, final):
    blk = MOBA_BLOCK
    a = ATTN_WIDTH
    i = pl.program_id(1)

    @pl.when(i == 0)
    def _():
        for n in range(nb):
            vt_ref[n] = vtf_ref[:, n * blk:(n + 1) * blk].astype(BF16)
            kblk = ktf_ref[:, n * blk:(n + 1) * blk].T
            kb_ref[n * blk:(n + 1) * blk, :] = kblk.astype(BF16)
            mean_n = jnp.mean(kblk, axis=0, keepdims=True)
            for h in range(N_HEADS):
                km_ref[h * nb + n:h * nb + n + 1, :] = jnp.where(_head_lane_mask(h, 1, a), mean_n, 0.0)

    q = q_ref[...]
    for h in range(N_HEADS):
        pair = q[:, _slab(h // 2)]
        qm_ref[h] = jnp.where(_head_lane_mask(h % 2, blk, LANES), pair, jnp.zeros_like(pair))

    def key_block(n, hp):
        return kb_ref[pl.ds(pl.multiple_of(n * blk, blk), blk), _slab(hp)]

    def scores(n, dst_ref):
        for h in range(N_HEADS):
            dst_ref[h] = _dot_nt(key_block(n, h // 2), qm_ref[h])

    def values(n):
        return [_dot(vt_ref[n, h * HEAD_DIM:(h + 1) * HEAD_DIM, :], p_ref[h]) for h in range(N_HEADS)]

    scores(i, s0_ref)
    scores(0, s1_ref)

    gate_t = _dot_nt(km_ref[...].astype(BF16), q)
    n_iota = lax.broadcasted_iota(jnp.int32, (nb, blk), 0)
    past = n_iota < i
    for h in range(N_HEADS):
        g = jnp.where(past, gate_t[h * nb:(h + 1) * nb, :], -jnp.inf)
        beaten = jnp.zeros((nb, blk), jnp.int32)
        for m in range(nb):
            gm = g[m:m + 1, :]
            beaten = beaten + ((gm > g) | ((gm == g) & (m < n_iota))).astype(jnp.int32)
        sel = past & (beaten < MOBA_TOPK)
        bias_ref[h * nb:(h + 1) * nb, :] = jnp.where(sel, 0.0, NEG_BIG)

    key_i = lax.broadcasted_iota(jnp.int32, (blk, blk), 0)
    qry_i = lax.broadcasted_iota(jnp.int32, (blk, blk), 1)
    causal = key_i <= qry_i
    for h in range(N_HEADS):
        s = jnp.where(causal, s0_ref[h], NEG_BIG)
        mx = jnp.max(s, axis=0, keepdims=True)
        p = jnp.exp2(s - mx)
        m_ref[h:h + 1, :] = mx
        l_ref[h:h + 1, :] = jnp.sum(p, axis=0, keepdims=True)
        p_ref[h] = p.astype(BF16)
    for h, pv in enumerate(values(i)):
        acc_ref[h * HEAD_DIM:(h + 1) * HEAD_DIM, :] = pv

    def past_step(n, cur_ref, nxt_ref):
        scores(n + 1, nxt_ref)
        alphas = []
        for h in range(N_HEADS):
            s = cur_ref[h]
            b = bias_ref[pl.ds(h * nb + n, 1), :]
            m_old = m_ref[h:h + 1, :]
            m_new = jnp.maximum(m_old, jnp.max(s, axis=0, keepdims=True) + b)
            alpha = jnp.exp2(m_old - m_new)
            p = jnp.exp2(s - (m_new - b))
            m_ref[h:h + 1, :] = m_new
            l_ref[h:h + 1, :] = alpha * l_ref[h:h + 1, :] + jnp.sum(p, axis=0, keepdims=True)
            p_ref[h] = p.astype(BF16)
            alphas.append(alpha)
        for h, pv in enumerate(values(n)):
            sl = slice(h * HEAD_DIM, (h + 1) * HEAD_DIM)
            acc_ref[sl, :] = alphas[h] * acc_ref[sl, :] + pv

    def past_block(n, carry):
        @pl.when(n % 2 == 0)
        def _():
            past_step(n, s1_ref, s0_ref)

        @pl.when(n % 2 == 1)
        def _():
            past_step(n, s0_ref, s1_ref)

        return carry

    lax.fori_loop(0, i, past_block, 0)

    for h in range(N_HEADS):
        sl = slice(h * HEAD_DIM, (h + 1) * HEAD_DIM)
        acc_ref[sl, :] = acc_ref[sl, :] / l_ref[h:h + 1, :]
    att = acc_ref[...].T
    o_ref[...] = _mix_out(att, ga_ref[...], mc_ref[...], x_ref[...], mod_ref[2:3, :], ag_ref[...], wo_ref,
                          fg_ref[...], final)


def _moba_prompt(q, kt_all, vt_all, layer, ga, mc, x, mod3, ag, wo_bf, fg, *, final):
    b, s, d = x.shape
    a = ATTN_WIDTH
    blk = MOBA_BLOCK
    assert s % blk == 0 and s // blk >= MOBA_TOPK
    nb = s // blk
    assert nb % 8 == 0
    tile = lambda w: pl.BlockSpec((None, blk, w), lambda bi, i: (bi, i, 0))
    const = lambda shape: pl.BlockSpec(shape, lambda bi, i: (0,) * len(shape))
    return pl.pallas_call(
        functools.partial(_moba_kernel, nb=nb, final=final),
        grid=(b, nb),
        in_specs=[tile(a),
                  pl.BlockSpec((None, None, a, s), lambda bi, i: (layer, bi, 0, 0)),
                  pl.BlockSpec((None, None, a, s), lambda bi, i: (layer, bi, 0, 0)),
                  tile(a), tile(a), tile(d),
                  pl.BlockSpec((None, 3, d), lambda bi, i: (bi, 0, 0)),
                  const((1, a)), const((2 * a, d)), const((1, d))],
        out_specs=tile(d),
        out_shape=jax.ShapeDtypeStruct((b, s, d), F32),
        scratch_shapes=[
            pltpu.VMEM((s, a), BF16),
            pltpu.VMEM((nb, a, blk), BF16),
            pltpu.VMEM((N_HEADS * nb, a), F32),
            pltpu.VMEM((N_HEADS, blk, LANES), BF16),
            pltpu.VMEM((N_HEADS * nb, blk), F32),
            pltpu.VMEM((N_HEADS, blk), F32),
            pltpu.VMEM((N_HEADS, blk), F32),
            pltpu.VMEM((a, blk), F32),
            pltpu.VMEM((N_HEADS, blk, blk), F32),
            pltpu.VMEM((N_HEADS, blk, blk), F32),
            pltpu.VMEM((N_HEADS, blk, blk), BF16),
        ],
        compiler_params=_params("arbitrary", "arbitrary"),
        name="moba",
    )(q, kt_all, vt_all, ga, mc, x, mod3, ag.reshape(1, a), wo_bf, fg.reshape(1, d))


def _paged_kernel(pt_ref, ck_ref, cv_ref, q_ref, kn_ref, vn_ref, o_ref,
                  pages_ref, sem_ref, qbd_ref, sc_ref, gate_ref, bmax_ref, idx_ref, m_ref, so_ref, l_ref, acc_ref,
                  *, n_steps, page, t_new, layer):
    pg = PAGES_PER_STEP
    a = ATTN_WIDTH
    rows = N_HEADS * t_new
    blk = MOBA_BLOCK
    ppb = blk // page
    bps = pg // ppb
    steps = 2 * n_steps
    seq = pl.program_id(0)
    t = pl.program_id(1)
    n_seq = pl.num_programs(0)
    slot = t % 2
    lane_n = lax.broadcasted_iota(jnp.int32, (rows, LANES), 1)

    def page_copy(cache_ref, sq, chunk, jp, dst_slot):
        src = cache_ref.at[layer, pt_ref[sq, chunk * pg + jp]]
        return pltpu.make_async_copy(src, pages_ref.at[dst_slot, jp], sem_ref.at[dst_slot])

    def start_step(sq, tt, dst_slot):
        @pl.when(tt < n_steps)
        def _():
            for jp in range(pg):
                page_copy(ck_ref, sq, tt, jp, dst_slot).start(priority=jp % 2)

        @pl.when(tt >= n_steps)
        def _():
            for jp in range(pg):
                page_copy(cv_ref, sq, tt - n_steps, jp, dst_slot).start(priority=jp % 2)

    @pl.when((seq == 0) & (t == 0))
    def _():
        start_step(seq, t, slot)

    last = (seq == n_seq - 1) & (t == steps - 1)

    @pl.when(jnp.logical_not(last))
    def _():
        wrap = t == steps - 1
        start_step(jnp.where(wrap, seq + 1, seq), jnp.where(wrap, 0, t + 1), 1 - slot)

    for jp in range(pg):
        page_copy(ck_ref, 0, 0, jp, slot).wait()

    @pl.when(t == 0)
    def _():
        q = q_ref[...].astype(F32)
        for h in range(N_HEADS):
            qbd_ref[h * t_new:(h + 1) * t_new, :] = jnp.where(_head_lane_mask(h, t_new, a), q, 0.0)
        gate_ref[...] = jnp.full((rows, LANES), -jnp.inf, F32)
        bmax_ref[...] = jnp.full((rows, LANES), -jnp.inf, F32)

    @pl.when(t < n_steps)
    def _():
        qbd = qbd_ref[...].astype(BF16)
        gate = gate_ref[...]
        bmax = bmax_ref[...]
        for bi in range(bps):
            tot = None
            top = None
            for c in range(ppb):
                jp = bi * ppb + c
                s = _dot(qbd, pages_ref[slot, jp].astype(BF16))
                sc_ref[t * pg + jp] = s
                tot = s if tot is None else tot + s
                top = s if top is None else jnp.maximum(top, s)
            n = t * bps + bi
            gate = jnp.where(lane_n == n, jnp.sum(tot, axis=1, keepdims=True), gate)
            bmax = jnp.where(lane_n == n, jnp.max(top, axis=1, keepdims=True), bmax)
        gate_ref[...] = gate
        bmax_ref[...] = bmax

    @pl.when(t == n_steps - 1)
    def _():
        gate = gate_ref[...]
        sel = jnp.zeros((rows, LANES), jnp.bool_)
        for r in range(MOBA_TOPK):
            mx = jnp.max(gate, axis=1, keepdims=True)
            first = jnp.min(jnp.where(gate == mx, lane_n, LANES), axis=1, keepdims=True)
            pick = lane_n == first
            idx_ref[r] = jnp.broadcast_to(first, (rows, LANES))
            sel = sel | pick
            gate = jnp.where(pick, -jnp.inf, gate)
        kn = jnp.concatenate([kn_ref[...], jnp.zeros((LANES - t_new, a), F32)], axis=0).astype(BF16)
        s_own = _dot_nt(qbd_ref[...].astype(BF16), kn)
        row_t = lax.broadcasted_iota(jnp.int32, (rows, LANES), 0) % t_new
        s_own = jnp.where(lane_n <= row_t, s_own, NEG_BIG)
        mx = jnp.maximum(jnp.max(s_own, axis=1, keepdims=True),
                         jnp.max(jnp.where(sel, bmax_ref[...], NEG_BIG), axis=1, keepdims=True))
        p_own = jnp.exp2(s_own - mx)
        m_ref[...] = jnp.broadcast_to(mx, (rows, LANES))
        so_ref[...] = p_own
        l_ref[...] = p_own
        acc_ref[...] = jnp.zeros((rows, a), F32)

    @pl.when(t >= n_steps)
    def _():
        c0 = t - n_steps
        mx = m_ref[...]
        picks = [idx_ref[r] for r in range(MOBA_TOPK)]
        acc = acc_ref[...]
        lpart = l_ref[...]
        for jp in range(pg):
            n = c0 * bps + jp // ppb
            keep = picks[0] == n
            for r in range(1, MOBA_TOPK):
                keep = keep | (picks[r] == n)
            p = jnp.where(keep, jnp.exp2(sc_ref[c0 * pg + jp] - mx), 0.0)
            lpart = lpart + p
            acc = acc + _dot_nt(p.astype(BF16), pages_ref[slot, jp].astype(BF16))
        acc_ref[...] = acc
        l_ref[...] = lpart

    @pl.when(t == steps - 1)
    def _():
        vn = jnp.concatenate([vn_ref[...], jnp.zeros((LANES - t_new, a), F32)], axis=0).astype(BF16)
        lsum = jnp.sum(l_ref[...], axis=1, keepdims=True)
        o = (acc_ref[...] + _dot(so_ref[...].astype(BF16), vn)) / lsum
        att = jnp.zeros((t_new, a), F32)
        for h in range(N_HEADS):
            att = att + jnp.where(_head_lane_mask(h, t_new, a), o[h * t_new:(h + 1) * t_new, :], 0.0)
        o_ref[...] = att


def _paged_attention(q, kn, vn, cache_kt, cache_vt, page_table, layer):
    nseq, t_new, a = q.shape
    n_pages = page_table.shape[1]
    page = cache_kt.shape[3]
    pg = PAGES_PER_STEP
    assert n_pages % pg == 0 and MOBA_BLOCK % page == 0 and page == LANES and t_new == 8
    assert pg % (MOBA_BLOCK // page) == 0
    n_steps = n_pages // pg
    n_keys = n_pages * page
    assert n_keys // MOBA_BLOCK <= LANES and n_keys // MOBA_BLOCK >= MOBA_TOPK
    rows = N_HEADS * t_new

    seq = lambda: pl.BlockSpec((None, t_new, a), lambda s, t, pt: (s, 0, 0))
    grid_spec = pltpu.PrefetchScalarGridSpec(
        num_scalar_prefetch=1,
        grid=(nseq, 2 * n_steps),
        in_specs=[pl.BlockSpec(memory_space=pl.ANY), pl.BlockSpec(memory_space=pl.ANY), seq(), seq(), seq()],
        out_specs=seq(),
        scratch_shapes=[
            pltpu.VMEM((2, pg, a, page), F32),
            pltpu.SemaphoreType.DMA((2,)),
            pltpu.VMEM((rows, a), F32),
            pltpu.VMEM((n_pages, rows, page), F32),
            pltpu.VMEM((rows, LANES), F32),
            pltpu.VMEM((rows, LANES), F32),
            pltpu.VMEM((MOBA_TOPK, rows, LANES), jnp.int32),
            pltpu.VMEM((rows, LANES), F32),
            pltpu.VMEM((rows, LANES), F32),
            pltpu.VMEM((rows, LANES), F32),
            pltpu.VMEM((rows, a), F32),
        ],
    )
    return pl.pallas_call(
        functools.partial(_paged_kernel, n_steps=n_steps, page=page, t_new=t_new, layer=layer),
        grid_spec=grid_spec,
        out_shape=jax.ShapeDtypeStruct((nseq, t_new, a), F32),
        compiler_params=_params("arbitrary", "arbitrary"),
        name="paged",
    )(page_table, cache_kt, cache_vt, q, kn, vn)


def _rope_tables(pos):
    half = HEAD_DIM // 2
    inv = ROPE_THETA ** (-jnp.arange(half, dtype=F32) * (2.0 / HEAD_DIM))
    ang = pos.astype(F32)[:, None] * inv[None, :]
    cos = jnp.cos(ang)
    sin = jnp.sin(ang)
    cos_t = jnp.tile(cos, (1, LANES // half))
    sin_t = jnp.tile(jnp.concatenate([-sin, sin], axis=1), (1, LANES // HEAD_DIM))
    return cos_t, sin_t, cos.T, sin.T


def _heads_last(xt, bp, sp):
    depth = xt.shape[0]
    return jnp.transpose(xt.reshape(depth, bp, N_HEADS, HEAD_DIM, sp), (0, 1, 4, 2, 3))


def kernel(x_prompt, x_sample, c_prompt, c_sample, cache_k, cache_v, state_conv, page_table, norm_gain, w_ada, b_ada,
           w_in, conv_dw_w, conv_dw_b, conv_ln_g, conv_ln_b, conv_pw2, attn_out_gain, conv_out_gain, w_out,
           final_gain):
    depth = w_in.shape[0]
    bp, sp, d = x_prompt.shape
    bs, ss, _ = x_sample.shape
    a = ATTN_WIDTH
    n_pool, page = cache_k.shape[1], cache_k.shape[2]
    past_len = page_table.shape[1] * page

    tab_p = _rope_tables(jnp.arange(sp, dtype=jnp.int32))
    tab_s = _rope_tables(past_len + jnp.arange(ss, dtype=jnp.int32))
    mod = _ada(jnp.concatenate([c_prompt, c_sample], axis=0), w_ada, b_ada)
    mod = mod.reshape(depth, bp + bs, 3, d)
    ckt = jnp.transpose(cache_k, (0, 1, 3, 4, 2)).reshape(depth, n_pool, a, page)
    cvt = jnp.transpose(cache_v, (0, 1, 3, 4, 2)).reshape(depth, n_pool, a, page)
    wi_bf = w_in.astype(BF16)
    wkv_t = jnp.transpose(w_in[:, :, a:3 * a], (0, 2, 1)).astype(BF16)
    pw2_bf = conv_pw2.astype(BF16)
    wo_bf = w_out.astype(BF16)
    zero_hist = jnp.zeros((bp, CONV_HIST, a), F32)

    hp, hs = x_prompt, x_sample
    kv_prev = None
    cp_l, ks_l, vs_l, cs_l = [], [], [], []
    for l in range(depth):
        final = l == depth - 1
        conv_prm = (conv_dw_w[l], conv_dw_b[l], conv_ln_g[l], conv_ln_b[l], pw2_bf[l], conv_out_gain[l])
        mod_p, mod_s = mod[l, :bp], mod[l, bp:]

        q, kt_all, vt_all, ga, mc, cst = _inproj(hp, mod_p, norm_gain[l], wi_bf[l], tab_p, zero_hist, *conv_prm,
                                                 tm=256, wkv_t=wkv_t[l], layer=l, depth=depth, kv_prev=kv_prev)
        kv_prev = (kt_all, vt_all)
        hp = _moba_prompt(q, kt_all, vt_all, l, ga, mc, hp, mod_p, attn_out_gain[l], wo_bf[l], final_gain,
                          final=final)
        cp_l.append(cst)

        q, k, v, ga, mc, cst = _inproj(hs, mod_s, norm_gain[l], wi_bf[l], tab_s, state_conv[l], *conv_prm, tm=ss)
        att = _paged_attention(q, k, v, ckt, cvt, page_table, l)
        hs = _outproj(att, ga, mc, hs, mod_s, attn_out_gain[l], wo_bf[l], final_gain, final=final)
        ks_l.append(k.reshape(bs, ss, N_HEADS, HEAD_DIM))
        vs_l.append(v.reshape(bs, ss, N_HEADS, HEAD_DIM))
        cs_l.append(cst)

    kt_all, vt_all = kv_prev
    return (hp, hs, _heads_last(kt_all, bp, sp), _heads_last(vt_all, bp, sp), jnp.stack(cp_l),
            jnp.stack(ks_l), jnp.stack(vs_l), jnp.stack(cs_l))
```

```python
import functools
import math

import jax
import jax.numpy as jnp
from jax import lax
from jax.experimental import pallas as pl
from jax.experimental.pallas import tpu as pltpu

N_HEADS = 8
HEAD_DIM = 64
ATTN_WIDTH = N_HEADS * HEAD_DIM
MOBA_BLOCK = 256
MOBA_TOPK = 3
CONV_K = 31
CONV_HIST = CONV_K - 1
ROPE_THETA = 10000.0
NORM_EPS = 1e-6

LANES = 128
N_SLABS = ATTN_WIDTH // LANES
HIST_PAD = 32
NEG_BIG = -1e30
VMEM_LIMIT_BYTES = 56 * 1024 * 1024
PAGES_PER_STEP = 16
PAGE_SLOTS = 4
Q_SCALE = math.log2(math.e) / math.sqrt(HEAD_DIM)

F32 = jnp.float32
BF16 = jnp.bfloat16


def _silu(x):
    return x * jax.nn.sigmoid(x)


def _dot(a, b):
    return jnp.dot(a, b, preferred_element_type=F32)


def _dot_nt(a, b):
    return lax.dot_general(a, b, (((1,), (1,)), ((), ())), preferred_element_type=F32)


def _params(*semantics, flags=None):
    return pltpu.CompilerParams(dimension_semantics=semantics, vmem_limit_bytes=VMEM_LIMIT_BYTES, flags=flags)


def _slab(c):
    return slice(c * LANES, (c + 1) * LANES)


def _ada_kernel(c_ref, wa_ref, ba_ref, o_ref):
    s = _silu(c_ref[...]).astype(BF16)
    o_ref[...] = _dot(s, wa_ref[...].astype(BF16)) + ba_ref[...]


def _ada(c_all, w_ada, b_ada):
    depth, d, n3 = w_ada.shape
    rows = c_all.shape[0]
    tn = 512
    return pl.pallas_call(
        _ada_kernel,
        grid=(depth, n3 // tn),
        in_specs=[
            pl.BlockSpec((rows, d), lambda l, n: (0, 0)),
            pl.BlockSpec((None, d, tn), lambda l, n: (l, 0, n)),
            pl.BlockSpec((None, 1, tn), lambda l, n: (l, 0, n)),
        ],
        out_specs=pl.BlockSpec((None, rows, tn), lambda l, n: (l, 0, n)),
        out_shape=jax.ShapeDtypeStruct((depth, rows, n3), F32),
        compiler_params=_params("arbitrary", "arbitrary"),
        name="ada",
    )(c_all, w_ada, b_ada.reshape(depth, 1, n3))


def _rope_tile(z, cos, sin_signed):
    lane = lax.broadcasted_iota(jnp.int32, (z.shape[0], LANES), 1)
    first_half = (lane % HEAD_DIM) < (HEAD_DIM // 2)
    outs = []
    for c in range(z.shape[1] // LANES):
        zc = z[:, _slab(c)]
        swapped = jnp.where(first_half,
                            pltpu.roll(zc, LANES - HEAD_DIM // 2, 1),
                            pltpu.roll(zc, HEAD_DIM // 2, 1))
        outs.append(zc * cos + swapped * sin_signed)
    return jnp.concatenate(outs, axis=1)


def _rope_tile_t(zt, cos_t, sin_t):
    half = HEAD_DIM // 2
    outs = []
    for h in range(N_HEADS):
        x1 = zt[h * HEAD_DIM:h * HEAD_DIM + half, :]
        x2 = zt[h * HEAD_DIM + half:(h + 1) * HEAD_DIM, :]
        outs.append(x1 * cos_t - x2 * sin_t)
        outs.append(x2 * cos_t + x1 * sin_t)
    return jnp.concatenate(outs, axis=0)


def _layernorm_swish(acc, lng, lnb):
    mu = jnp.mean(acc, axis=-1, keepdims=True)
    cen = acc - mu
    var = jnp.mean(cen * cen, axis=-1, keepdims=True)
    return _silu(cen * lax.rsqrt(var + NORM_EPS) * lng + lnb)


def _inproj_kernel(*refs, tm, rc, kv_t, fill_layers):
    (x_ref, mod_ref, ng_ref, wi_ref, cos_ref, sin_ref, hist_ref, dww_ref, dwb_ref, lng_ref, lnb_ref, pw2_ref,
     cg_ref) = refs[:13]
    refs = refs[13:]
    if kv_t:
        wkv_ref, cost_ref, sint_ref = refs[:3]
        refs = refs[3:]
        if len(refs) == 12:
            refs = refs[2:]
        q_ref, kt_ref, vt_ref, ga_ref, mc_ref, cs_ref, h_ref, xp_ref, gc_ref, sw_ref = refs
    else:
        q_ref, k_ref, v_ref, ga_ref, mc_ref, cs_ref, h_ref, xp_ref, gc_ref, sw_ref = refs
    a = ATTN_WIDTH
    off = HIST_PAD - CONV_HIST
    j = pl.program_id(1)

    @pl.when(j == 0)
    def _():
        hist = hist_ref[...]
        for c in range(N_SLABS):
            xp_ref[c, 0:off, :] = jnp.zeros((off, LANES), F32)
            xp_ref[c, off:HIST_PAD, :] = hist[:, _slab(c)]

    x = x_ref[...]
    shift = mod_ref[0:1, :]
    scale = mod_ref[1:2, :]
    xn = x * lax.rsqrt(jnp.mean(x * x, axis=-1, keepdims=True) + NORM_EPS)
    h_ref[...] = ((xn * ng_ref[...]) * (1.0 + scale) + shift).astype(BF16)

    def proj(g):
        return _dot(h_ref[...], wi_ref[:, g * a:(g + 1) * a])

    glu = proj(4) * jax.nn.sigmoid(proj(5))
    for c in range(N_SLABS):
        xp_ref[c, HIST_PAD:HIST_PAD + tm, :] = glu[:, _slab(c)]
    gc_ref[...] = _silu(proj(6))

    cos = cos_ref[...]
    sin = sin_ref[...]

    def emit_q():
        q_ref[...] = (_rope_tile(proj(0), cos, sin) * Q_SCALE).astype(BF16)

    def emit_ga():
        ga_ref[...] = _silu(proj(3))

    if kv_t:
        def put_t(dst_ref, val):
            if fill_layers is None:
                dst_ref[...] = val
            else:
                own, depth = fill_layers
                for o in range(depth):
                    dst_ref[o] = val if o == own else jnp.zeros_like(val)

        def emit_kt():
            put_t(kt_ref, _rope_tile_t(_dot_nt(wkv_ref[0:a, :], h_ref[...]), cost_ref[...], sint_ref[...]))

        def emit_vt():
            put_t(vt_ref, _dot_nt(wkv_ref[a:2 * a, :], h_ref[...]))

        matmul_tasks = [emit_q, emit_kt, emit_vt, emit_ga]
    else:
        def emit_k():
            k_ref[...] = _rope_tile(proj(1), cos, sin)

        def emit_v():
            v_ref[...] = proj(2)

        matmul_tasks = [emit_q, emit_k, emit_v, emit_ga]

    def conv_slab(c):
        bias = dwb_ref[:, _slab(c)]
        if kv_t:
            half = tm // 2
            for parity in range(2):
                acc = jnp.zeros((half, LANES), F32) + bias
                for tap in range(CONV_K):
                    win = xp_ref[c, pl.ds(off + parity + tap, half, stride=2), :]
                    acc = acc + win * dww_ref[tap:tap + 1, _slab(c)]
                sw_ref[c, pl.ds(parity, half, stride=2), :] = acc
        else:
            acc = jnp.zeros((tm, LANES), F32) + bias
            for tap in range(CONV_K):
                acc = acc + xp_ref[c, off + tap:off + tap + tm, :] * dww_ref[tap:tap + 1, _slab(c)]
            sw_ref[c, :, :] = acc

    conv_tasks = [functools.partial(conv_slab, c) for c in range(N_SLABS)]

    for task in range(max(len(conv_tasks), len(matmul_tasks))):
        if task < len(conv_tasks):
            conv_tasks[task]()
        if task < len(matmul_tasks):
            matmul_tasks[task]()

    lng = lng_ref[...]
    lnb = lnb_ref[...]
    for r0 in range(0, tm, rc):
        dw = jnp.concatenate([sw_ref[c, r0:r0 + rc, :] for c in range(N_SLABS)], axis=1)
        y = _layernorm_swish(dw, lng, lnb)
        for c in range(N_SLABS):
            sw_ref[c, r0:r0 + rc, :] = y[:, _slab(c)]

    sw = jnp.concatenate([sw_ref[c] for c in range(N_SLABS)], axis=1).astype(BF16)
    co = _dot(sw, pw2_ref[...])
    con = co * lax.rsqrt(jnp.mean(co * co, axis=-1, keepdims=True) + NORM_EPS) * cg_ref[...]
    mc_ref[...] = (con * gc_ref[...]).astype(BF16)

    cs_ref[...] = jnp.concatenate([xp_ref[c, HIST_PAD + tm - CONV_HIST:HIST_PAD + tm, :] for c in range(N_SLABS)],
                                  axis=1)
    for c in range(N_SLABS):
        tail = xp_ref[c, tm:tm + HIST_PAD, :]
        xp_ref[c, 0:HIST_PAD, :] = tail


def _inproj(x, mod3, ng, wi_bf, tables, hist, dww, dwb, lng, lnb, pw2_bf, cg, *, tm, wkv_t=None, layer=0, depth=1,
            kv_prev=None):
    b, s, d = x.shape
    a = ATTN_WIDTH
    n_in = wi_bf.shape[1]
    kv_t = wkv_t is not None
    rc = 64 if kv_t else min(tm, 32)
    assert s % tm == 0 and tm % rc == 0 and (not kv_t or tm % (2 * rc) == 0)
    cos_t, sin_t, cos_tt, sin_tt = tables
    row = lambda v: v.reshape(1, -1)
    const = lambda shape: pl.BlockSpec(shape, lambda bi, j: (0,) * len(shape))
    tile = lambda w: pl.BlockSpec((None, tm, w), lambda bi, j: (bi, j, 0))
    if kv_prev is None:
        tile_t = pl.BlockSpec((depth, None, a, tm), lambda bi, j: (0, bi, 0, j))
        fill_layers = (layer, depth)
    else:
        tile_t = pl.BlockSpec((None, None, a, tm), lambda bi, j: (layer, bi, 0, j))
        fill_layers = None
    hist_spec = pl.BlockSpec((None, CONV_HIST, a), lambda bi, j: (bi, 0, 0))
    in_specs = [
        tile(d),
        pl.BlockSpec((None, 3, d), lambda bi, j: (bi, 0, 0)),
        const((1, d)),
        const((d, n_in)),
        pl.BlockSpec((tm, LANES), lambda bi, j: (j, 0)),
        pl.BlockSpec((tm, LANES), lambda bi, j: (j, 0)),
        hist_spec,
        const((CONV_K, a)),
        const((1, a)),
        const((1, a)),
        const((1, a)),
        const((a, a)),
        const((1, a)),
    ]
    args = [x, mod3, row(ng), wi_bf, cos_t, sin_t, hist, dww, row(dwb), row(lng), row(lnb), pw2_bf, row(cg)]
    q_shape = jax.ShapeDtypeStruct((b, s, a), BF16)
    tail_specs = [tile(a), tile(a), hist_spec]
    tail_shapes = [
        jax.ShapeDtypeStruct((b, s, a), F32),
        jax.ShapeDtypeStruct((b, s, a), BF16),
        jax.ShapeDtypeStruct((b, CONV_HIST, a), F32),
    ]
    aliases = {}
    if kv_t:
        half = HEAD_DIM // 2
        in_specs += [const((2 * a, d)),
                     pl.BlockSpec((half, tm), lambda bi, j: (0, j)),
                     pl.BlockSpec((half, tm), lambda bi, j: (0, j))]
        args += [wkv_t, cos_tt, sin_tt]
        if kv_prev is not None:
            aliases = {len(args): 1, len(args) + 1: 2}
            in_specs += [pl.BlockSpec(memory_space=pl.ANY)] * 2
            args += list(kv_prev)
        out_specs = [tile(a), tile_t, tile_t] + tail_specs
        out_shape = [q_shape,
                     jax.ShapeDtypeStruct((depth, b, a, s), F32),
                     jax.ShapeDtypeStruct((depth, b, a, s), F32),
                     ] + tail_shapes
    else:
        out_specs = [tile(a), tile(a), tile(a)] + tail_specs
        out_shape = [q_shape,
                     jax.ShapeDtypeStruct((b, s, a), F32),
                     jax.ShapeDtypeStruct((b, s, a), F32),
                     ] + tail_shapes
    return pl.pallas_call(
        functools.partial(_inproj_kernel, tm=tm, rc=rc, kv_t=kv_t, fill_layers=fill_layers if kv_t else None),
        grid=(b, s // tm),
        in_specs=in_specs,
        out_specs=out_specs,
        out_shape=out_shape,
        input_output_aliases=aliases,
        scratch_shapes=[
            pltpu.VMEM((tm, d), BF16),
            pltpu.VMEM((N_SLABS, HIST_PAD + tm, LANES), F32),
            pltpu.VMEM((tm, a), F32),
            pltpu.VMEM((N_SLABS, tm, LANES), F32),
        ],
        compiler_params=_params("arbitrary", "arbitrary"),
        name="inproj_t" if kv_t else "inproj",
    )(*args)


def _mix_out(att, ga, mc_bf, x, gate, ag, wo_ref, fg, final):
    a = ATTN_WIDTH
    an = att * lax.rsqrt(jnp.mean(att * att, axis=-1, keepdims=True) + NORM_EPS) * ag
    ma = (an * ga).astype(BF16)
    y = _dot(ma, wo_ref[0:a, :]) + _dot(mc_bf, wo_ref[a:2 * a, :])
    xn = x + gate * y
    if final:
        xn = xn * lax.rsqrt(jnp.mean(xn * xn, axis=-1, keepdims=True) + NORM_EPS) * fg
    return xn


def _outproj_kernel(att_ref, ga_ref, mc_ref, x_ref, mod_ref, ag_ref, wo_ref, fg_ref, o_ref, *, final):
    o_ref[...] = _mix_out(att_ref[...], ga_ref[...], mc_ref[...], x_ref[...], mod_ref[2:3, :], ag_ref[...], wo_ref,
                          fg_ref[...], final)


def _outproj(att, ga, mc, x, mod3, ag, wo_bf, fg, *, final):
    b, s, d = x.shape
    a = ATTN_WIDTH
    tile = lambda w: pl.BlockSpec((None, s, w), lambda bi: (bi, 0, 0))
    const = lambda shape: pl.BlockSpec(shape, lambda bi: (0,) * len(shape))
    return pl.pallas_call(
        functools.partial(_outproj_kernel, final=final),
        grid=(b,),
        in_specs=[tile(a), tile(a), tile(a), tile(d), pl.BlockSpec((None, 3, d), lambda bi: (bi, 0, 0)),
                  const((1, a)), const((2 * a, d)), const((1, d))],
        out_specs=tile(d),
        out_shape=jax.ShapeDtypeStruct((b, s, d), F32),
        compiler_params=_params("arbitrary"),
        name="outproj",
    )(att, ga, mc, x, mod3, ag.reshape(1, a), wo_bf, fg.reshape(1, d))


def _head_lane_mask(h, rows, width):
    lane = lax.broadcasted_iota(jnp.int32, (rows, width), 1)
    return (lane // HEAD_DIM) == h


def _moba_kernel(q_ref, ktf_ref, vtf_ref, ga_ref, mc_ref, x_ref, mod_ref, ag_ref, wo_ref, fg_ref, o_ref,
                 kb_ref, vt_ref, km_ref, qm_ref, bias_ref, m_ref, l_ref, acc_ref, s0_ref, s1_ref, p_ref,
                 *, nb, final):
    blk = MOBA_BLOCK
    a = ATTN_WIDTH
    i = pl.program_id(1)

    @pl.when(i == 0)
    def _():
        for n in range(nb):
            vt_ref[n] = vtf_ref[:, n * blk:(n + 1) * blk].astype(BF16)
            kblk = ktf_ref[:, n * blk:(n + 1) * blk].T
            kb_ref[n * blk:(n + 1) * blk, :] = kblk.astype(BF16)
            mean_n = jnp.mean(kblk, axis=0, keepdims=True)
            for h in range(N_HEADS):
                km_ref[h * nb + n:h * nb + n + 1, :] = jnp.where(_head_lane_mask(h, 1, a), mean_n, 0.0)

    q = q_ref[...]
    for h in range(N_HEADS):
        pair = q[:, _slab(h // 2)]
        qm_ref[h] = jnp.where(_head_lane_mask(h % 2, blk, LANES), pair, jnp.zeros_like(pair))

    def key_block(n, hp):
        return kb_ref[pl.ds(pl.multiple_of(n * blk, blk), blk), _slab(hp)]

    def scores(n, dst_ref):
        for h in range(N_HEADS):
            dst_ref[h] = _dot_nt(key_block(n, h // 2), qm_ref[h])

    def values(n):
        return [_dot(vt_ref[n, h * HEAD_DIM:(h + 1) * HEAD_DIM, :], p_ref[h]) for h in range(N_HEADS)]

    scores(i, s0_ref)
    scores(0, s1_ref)

    gate_t = _dot_nt(km_ref[...].astype(BF16), q)
    n_iota = lax.broadcasted_iota(jnp.int32, (nb, blk), 0)
    past = n_iota < i
    for h in range(N_HEADS):
        g = jnp.where(past, gate_t[h * nb:(h + 1) * nb, :], -jnp.inf)
        beaten = jnp.zeros((nb, blk), jnp.int32)
        for m in range(nb):
            gm = g[m:m + 1, :]
            beaten = beaten + ((gm > g) | ((gm == g) & (m < n_iota))).astype(jnp.int32)
        sel = past & (beaten < MOBA_TOPK)
        bias_ref[h * nb:(h + 1) * nb, :] = jnp.where(sel, 0.0, NEG_BIG)

    key_i = lax.broadcasted_iota(jnp.int32, (blk, blk), 0)
    qry_i = lax.broadcasted_iota(jnp.int32, (blk, blk), 1)
    causal = key_i <= qry_i
    for h in range(N_HEADS):
        s = jnp.where(causal, s0_ref[h], NEG_BIG)
        mx = jnp.max(s, axis=0, keepdims=True)
        p = jnp.exp2(s - mx)
        m_ref[h:h + 1, :] = mx
        l_ref[h:h + 1, :] = jnp.sum(p, axis=0, keepdims=True)
        p_ref[h] = p.astype(BF16)
    for h, pv in enumerate(values(i)):
        acc_ref[h * HEAD_DIM:(h + 1) * HEAD_DIM, :] = pv

    def past_step(n, cur_ref, nxt_ref):
        scores(n + 1, nxt_ref)
        alphas = []
        for h in range(N_HEADS):
            s = cur_ref[h]
            b = bias_ref[pl.ds(h * nb + n, 1), :]
            m_old = m_ref[h:h + 1, :]
            m_new = jnp.maximum(m_old, jnp.max(s, axis=0, keepdims=True) + b)
            alpha = jnp.exp2(m_old - m_new)
            p = jnp.exp2(s - (m_new - b))
            m_ref[h:h + 1, :] = m_new
            l_ref[h:h + 1, :] = alpha * l_ref[h:h + 1, :] + jnp.sum(p, axis=0, keepdims=True)
            p_ref[h] = p.astype(BF16)
            alphas.append(alpha)
        for h, pv in enumerate(values(n)):
            sl = slice(h * HEAD_DIM, (h + 1) * HEAD_DIM)
            acc_ref[sl, :] = alphas[h] * acc_ref[sl, :] + pv

    def past_block(n, carry):
        @pl.when(n % 2 == 0)
        def _():
            past_step(n, s1_ref, s0_ref)

        @pl.when(n % 2 == 1)
        def _():
            past_step(n, s0_ref, s1_ref)

        return carry

    lax.fori_loop(0, i, past_block, 0)

    for h in range(N_HEADS):
        sl = slice(h * HEAD_DIM, (h + 1) * HEAD_DIM)
        acc_ref[sl, :] = acc_ref[sl, :] / l_ref[h:h + 1, :]
    att = acc_ref[...].T
    o_ref[...] = _mix_out(att, ga_ref[...], mc_ref[...], x_ref[...], mod_ref[2:3, :], ag_ref[...], wo_ref,
                          fg_ref[...], final)


def _moba_prompt(q, kt_all, vt_all, layer, ga, mc, x, mod3, ag, wo_bf, fg, *, final):
    b, s, d = x.shape
    a = ATTN_WIDTH
    blk = MOBA_BLOCK
    assert s % blk == 0 and s // blk >= MOBA_TOPK
    nb = s // blk
    assert nb % 8 == 0
    tile = lambda w: pl.BlockSpec((None, blk, w), lambda bi, i: (bi, i, 0))
    const = lambda shape: pl.BlockSpec(shape, lambda bi, i: (0,) * len(shape))
    return pl.pallas_call(
        functools.partial(_moba_kernel, nb=nb, final=final),
        grid=(b, nb),
        in_specs=[tile(a),
                  pl.BlockSpec((None, None, a, s), lambda bi, i: (layer, bi, 0, 0)),
                  pl.BlockSpec((None, None, a, s), lambda bi, i: (layer, bi, 0, 0)),
                  tile(a), tile(a), tile(d),
                  pl.BlockSpec((None, 3, d), lambda bi, i: (bi, 0, 0)),
                  const((1, a)), const((2 * a, d)), const((1, d))],
        out_specs=tile(d),
        out_shape=jax.ShapeDtypeStruct((b, s, d), F32),
        scratch_shapes=[
            pltpu.VMEM((s, a), BF16),
            pltpu.VMEM((nb, a, blk), BF16),
            pltpu.VMEM((N_HEADS * nb, a), F32),
            pltpu.VMEM((N_HEADS, blk, LANES), BF16),
            pltpu.VMEM((N_HEADS * nb, blk), F32),
            pltpu.VMEM((N_HEADS, blk), F32),
            pltpu.VMEM((N_HEADS, blk), F32),
            pltpu.VMEM((a, blk), F32),
            pltpu.VMEM((N_HEADS, blk, blk), F32),
            pltpu.VMEM((N_HEADS, blk, blk), F32),
            pltpu.VMEM((N_HEADS, blk, blk), BF16),
        ],
        compiler_params=_params("arbitrary", "arbitrary"),
        name="moba",
    )(q, kt_all, vt_all, ga, mc, x, mod3, ag.reshape(1, a), wo_bf, fg.reshape(1, d))


def _paged_kernel(pt_ref, ck_ref, cv_ref, q_ref, kn_ref, vn_ref, o_ref,
                  pages_ref, sem_ref, qbd_ref, sc_ref, gate_ref, bmax_ref, idx_ref, m_ref, so_ref, l_ref, acc_ref,
                  *, n_steps, page, t_new, layer):
    pg = PAGES_PER_STEP
    a = ATTN_WIDTH
    rows = N_HEADS * t_new
    blk = MOBA_BLOCK
    ppb = blk // page
    bps = pg // ppb
    steps = 2 * n_steps
    seq = pl.program_id(0)
    t = pl.program_id(1)
    n_seq = pl.num_programs(0)
    slot = t % PAGE_SLOTS
    ahead = PAGE_SLOTS - 1
    lane_n = lax.broadcasted_iota(jnp.int32, (rows, LANES), 1)

    def page_copy(cache_ref, sq, chunk, jp, dst_slot):
        src = cache_ref.at[layer, pt_ref[sq, chunk * pg + jp]]
        return pltpu.make_async_copy(src, pages_ref.at[dst_slot, jp], sem_ref.at[dst_slot])

    def start_step(sq, tt, dst_slot):
        @pl.when(tt < n_steps)
        def _():
            for jp in range(pg):
                page_copy(ck_ref, sq, tt, jp, dst_slot).start(priority=jp % 2)

        @pl.when(tt >= n_steps)
        def _():
            for jp in range(pg):
                page_copy(cv_ref, sq, tt - n_steps, jp, dst_slot).start(priority=jp % 2)

    @pl.when((seq == 0) & (t == 0))
    def _():
        for d in range(ahead):
            start_step(seq, d, d)

    later = t + ahead
    wrap = later >= steps

    @pl.when(jnp.logical_not(wrap & (seq == n_seq - 1)))
    def _():
        start_step(jnp.where(wrap, seq + 1, seq), jnp.where(wrap, later - steps, later), later % PAGE_SLOTS)

    for jp in range(pg):
        page_copy(ck_ref, 0, 0, jp, slot).wait()

    @pl.when(t == 0)
    def _():
        q = q_ref[...].astype(F32)
        for h in range(N_HEADS):
            qbd_ref[h * t_new:(h + 1) * t_new, :] = jnp.where(_head_lane_mask(h, t_new, a), q, 0.0)
        gate_ref[...] = jnp.full((rows, LANES), -jnp.inf, F32)
        bmax_ref[...] = jnp.full((rows, LANES), -jnp.inf, F32)

    @pl.when(t < n_steps)
    def _():
        qbd = qbd_ref[...].astype(BF16)
        gate = gate_ref[...]
        bmax = bmax_ref[...]
        for bi in range(bps):
            tot = None
            top = None
            for c in range(ppb):
                jp = bi * ppb + c
                s = _dot(qbd, pages_ref[slot, jp].astype(BF16))
                sc_ref[t * pg + jp] = s
                tot = s if tot is None else tot + s
                top = s if top is None else jnp.maximum(top, s)
            n = t * bps + bi
            gate = jnp.where(lane_n == n, jnp.sum(tot, axis=1, keepdims=True), gate)
            bmax = jnp.where(lane_n == n, jnp.max(top, axis=1, keepdims=True), bmax)
        gate_ref[...] = gate
        bmax_ref[...] = bmax

    @pl.when(t == n_steps - 1)
    def _():
        gate = gate_ref[...]
        sel = jnp.zeros((rows, LANES), jnp.bool_)
        for r in range(MOBA_TOPK):
            mx = jnp.max(gate, axis=1, keepdims=True)
            first = jnp.min(jnp.where(gate == mx, lane_n, LANES), axis=1, keepdims=True)
            pick = lane_n == first
            idx_ref[r] = jnp.broadcast_to(first, (rows, LANES))
            sel = sel | pick
            gate = jnp.where(pick, -jnp.inf, gate)
        kn = jnp.concatenate([kn_ref[...], jnp.zeros((LANES - t_new, a), F32)], axis=0).astype(BF16)
        s_own = _dot_nt(qbd_ref[...].astype(BF16), kn)
        row_t = lax.broadcasted_iota(jnp.int32, (rows, LANES), 0) % t_new
        s_own = jnp.where(lane_n <= row_t, s_own, NEG_BIG)
        mx = jnp.maximum(jnp.max(s_own, axis=1, keepdims=True),
                         jnp.max(jnp.where(sel, bmax_ref[...], NEG_BIG), axis=1, keepdims=True))
        p_own = jnp.exp2(s_own - mx)
        m_ref[...] = jnp.broadcast_to(mx, (rows, LANES))
        so_ref[...] = p_own
        l_ref[...] = p_own
        acc_ref[...] = jnp.zeros((rows, a), F32)

    @pl.when(t >= n_steps)
    def _():
        c0 = t - n_steps
        mx = m_ref[...]
        picks = [idx_ref[r] for r in range(MOBA_TOPK)]
        acc = acc_ref[...]
        lpart = l_ref[...]
        for jp in range(pg):
            n = c0 * bps + jp // ppb
            keep = picks[0] == n
            for r in range(1, MOBA_TOPK):
                keep = keep | (picks[r] == n)
            p = jnp.where(keep, jnp.exp2(sc_ref[c0 * pg + jp] - mx), 0.0)
            lpart = lpart + p
            acc = acc + _dot_nt(p.astype(BF16), pages_ref[slot, jp].astype(BF16))
        acc_ref[...] = acc
        l_ref[...] = lpart

    @pl.when(t == steps - 1)
    def _():
        vn = jnp.concatenate([vn_ref[...], jnp.zeros((LANES - t_new, a), F32)], axis=0).astype(BF16)
        lsum = jnp.sum(l_ref[...], axis=1, keepdims=True)
        o = (acc_ref[...] + _dot(so_ref[...].astype(BF16), vn)) / lsum
        att = jnp.zeros((t_new, a), F32)
        for h in range(N_HEADS):
            att = att + jnp.where(_head_lane_mask(h, t_new, a), o[h * t_new:(h + 1) * t_new, :], 0.0)
        o_ref[...] = att


def _paged_attention(q, kn, vn, cache_kt, cache_vt, page_table, layer):
    nseq, t_new, a = q.shape
    n_pages = page_table.shape[1]
    page = cache_kt.shape[3]
    pg = PAGES_PER_STEP
    assert n_pages % pg == 0 and MOBA_BLOCK % page == 0 and page == LANES and t_new == 8
    assert pg % (MOBA_BLOCK // page) == 0
    n_steps = n_pages // pg
    assert (2 * n_steps) % PAGE_SLOTS == 0 and 2 * n_steps >= PAGE_SLOTS
    n_keys = n_pages * page
    assert n_keys // MOBA_BLOCK <= LANES and n_keys // MOBA_BLOCK >= MOBA_TOPK
    rows = N_HEADS * t_new

    seq = lambda: pl.BlockSpec((None, t_new, a), lambda s, t, pt: (s, 0, 0))
    grid_spec = pltpu.PrefetchScalarGridSpec(
        num_scalar_prefetch=1,
        grid=(nseq, 2 * n_steps),
        in_specs=[pl.BlockSpec(memory_space=pl.ANY), pl.BlockSpec(memory_space=pl.ANY), seq(), seq(), seq()],
        out_specs=seq(),
        scratch_shapes=[
            pltpu.VMEM((PAGE_SLOTS, pg, a, page), F32),
            pltpu.SemaphoreType.DMA((PAGE_SLOTS,)),
            pltpu.VMEM((rows, a), F32),
            pltpu.VMEM((n_pages, rows, page), F32),
            pltpu.VMEM((rows, LANES), F32),
            pltpu.VMEM((rows, LANES), F32),
            pltpu.VMEM((MOBA_TOPK, rows, LANES), jnp.int32),
            pltpu.VMEM((rows, LANES), F32),
            pltpu.VMEM((rows, LANES), F32),
            pltpu.VMEM((rows, LANES), F32),
            pltpu.VMEM((rows, a), F32),
        ],
    )
    return pl.pallas_call(
        functools.partial(_paged_kernel, n_steps=n_steps, page=page, t_new=t_new, layer=layer),
        grid_spec=grid_spec,
        out_shape=jax.ShapeDtypeStruct((nseq, t_new, a), F32),
        compiler_params=_params("arbitrary", "arbitrary"),
        name="paged",
    )(page_table, cache_kt, cache_vt, q, kn, vn)


def _rope_tables(pos):
    half = HEAD_DIM // 2
    inv = ROPE_THETA ** (-jnp.arange(half, dtype=F32) * (2.0 / HEAD_DIM))
    ang = pos.astype(F32)[:, None] * inv[None, :]
    cos = jnp.cos(ang)
    sin = jnp.sin(ang)
    cos_t = jnp.tile(cos, (1, LANES // half))
    sin_t = jnp.tile(jnp.concatenate([-sin, sin], axis=1), (1, LANES // HEAD_DIM))
    return cos_t, sin_t, cos.T, sin.T


def _heads_last(xt, bp, sp):
    depth = xt.shape[0]
    return jnp.transpose(xt.reshape(depth, bp, N_HEADS, HEAD_DIM, sp), (0, 1, 4, 2, 3))


def kernel(x_prompt, x_sample, c_prompt, c_sample, cache_k, cache_v, state_conv, page_table, norm_gain, w_ada, b_ada,
           w_in, conv_dw_w, conv_dw_b, conv_ln_g, conv_ln_b, conv_pw2, attn_out_gain, conv_out_gain, w_out,
           final_gain):
    depth = w_in.shape[0]
    bp, sp, d = x_prompt.shape
    bs, ss, _ = x_sample.shape
    a = ATTN_WIDTH
    n_pool, page = cache_k.shape[1], cache_k.shape[2]
    past_len = page_table.shape[1] * page

    tab_p = _rope_tables(jnp.arange(sp, dtype=jnp.int32))
    tab_s = _rope_tables(past_len + jnp.arange(ss, dtype=jnp.int32))
    mod = _ada(jnp.concatenate([c_prompt, c_sample], axis=0), w_ada, b_ada)
    mod = mod.reshape(depth, bp + bs, 3, d)
    ckt = jnp.transpose(cache_k, (0, 1, 3, 4, 2)).reshape(depth, n_pool, a, page)
    cvt = jnp.transpose(cache_v, (0, 1, 3, 4, 2)).reshape(depth, n_pool, a, page)
    wi_bf = w_in.astype(BF16)
    wkv_t = jnp.transpose(w_in[:, :, a:3 * a], (0, 2, 1)).astype(BF16)
    pw2_bf = conv_pw2.astype(BF16)
    wo_bf = w_out.astype(BF16)
    zero_hist = jnp.zeros((bp, CONV_HIST, a), F32)

    hp, hs = x_prompt, x_sample
    kv_prev = None
    cp_l, ks_l, vs_l, cs_l = [], [], [], []
    for l in range(depth):
        final = l == depth - 1
        conv_prm = (conv_dw_w[l], conv_dw_b[l], conv_ln_g[l], conv_ln_b[l], pw2_bf[l], conv_out_gain[l])
        mod_p, mod_s = mod[l, :bp], mod[l, bp:]

        q, kt_all, vt_all, ga, mc, cst = _inproj(hp, mod_p, norm_gain[l], wi_bf[l], tab_p, zero_hist, *conv_prm,
                                                 tm=256, wkv_t=wkv_t[l], layer=l, depth=depth, kv_prev=kv_prev)
        kv_prev = (kt_all, vt_all)
        hp = _moba_prompt(q, kt_all, vt_all, l, ga, mc, hp, mod_p, attn_out_gain[l], wo_bf[l], final_gain,
                          final=final)
        cp_l.append(cst)

        q, k, v, ga, mc, cst = _inproj(hs, mod_s, norm_gain[l], wi_bf[l], tab_s, state_conv[l], *conv_prm, tm=ss)
        att = _paged_attention(q, k, v, ckt, cvt, page_table, l)
        hs = _outproj(att, ga, mc, hs, mod_s, attn_out_gain[l], wo_bf[l], final_gain, final=final)
        ks_l.append(k.reshape(bs, ss, N_HEADS, HEAD_DIM))
        vs_l.append(v.reshape(bs, ss, N_HEADS, HEAD_DIM))
        cs_l.append(cst)

    kt_all, vt_all = kv_prev
    return (hp, hs, _heads_last(kt_all, bp, sp), _heads_last(vt_all, bp, sp), jnp.stack(cp_l),
            jnp.stack(ks_l), jnp.stack(vs_l), jnp.stack(cs_l))
```

```python
import functools
import math

import jax
import jax.numpy as jnp
from jax import lax
from jax.experimental import pallas as pl
from jax.experimental.pallas import tpu as pltpu

N_HEADS = 8
HEAD_DIM = 64
ATTN_WIDTH = N_HEADS * HEAD_DIM
MOBA_BLOCK = 256
MOBA_TOPK = 3
CONV_K = 31
CONV_HIST = CONV_K - 1
ROPE_THETA = 10000.0
NORM_EPS = 1e-6

LANES = 128
N_SLABS = ATTN_WIDTH // LANES
HIST_PAD = 32
NEG_BIG = -1e30
VMEM_LIMIT_BYTES = 56 * 1024 * 1024
PAGES_PER_STEP = 16
PAGE_SLOTS = 4
Q_SCALE = math.log2(math.e) / math.sqrt(HEAD_DIM)

F32 = jnp.float32
BF16 = jnp.bfloat16


def _silu(x):
    return x * jax.nn.sigmoid(x)


def _dot(a, b):
    return jnp.dot(a, b, preferred_element_type=F32)


def _dot_nt(a, b):
    return lax.dot_general(a, b, (((1,), (1,)), ((), ())), preferred_element_type=F32)


def _params(*semantics):
    return pltpu.CompilerParams(dimension_semantics=semantics, vmem_limit_bytes=VMEM_LIMIT_BYTES)


def _slab(c):
    return slice(c * LANES, (c + 1) * LANES)


def _ada_kernel(c_ref, wa_ref, ba_ref, o_ref):
    s = _silu(c_ref[...]).astype(BF16)
    o_ref[...] = _dot(s, wa_ref[...].astype(BF16)) + ba_ref[...]


def _ada(c_all, w_ada, b_ada):
    depth, d, n3 = w_ada.shape
    rows = c_all.shape[0]
    tn = 512
    return pl.pallas_call(
        _ada_kernel,
        grid=(depth, n3 // tn),
        in_specs=[
            pl.BlockSpec((rows, d), lambda l, n: (0, 0)),
            pl.BlockSpec((None, d, tn), lambda l, n: (l, 0, n)),
            pl.BlockSpec((None, 1, tn), lambda l, n: (l, 0, n)),
        ],
        out_specs=pl.BlockSpec((None, rows, tn), lambda l, n: (l, 0, n)),
        out_shape=jax.ShapeDtypeStruct((depth, rows, n3), F32),
        compiler_params=_params("arbitrary", "arbitrary"),
        name="ada",
    )(c_all, w_ada, b_ada.reshape(depth, 1, n3))


def _rope_tile(z, cos, sin_signed):
    lane = lax.broadcasted_iota(jnp.int32, (z.shape[0], LANES), 1)
    first_half = (lane % HEAD_DIM) < (HEAD_DIM // 2)
    outs = []
    for c in range(z.shape[1] // LANES):
        zc = z[:, _slab(c)]
        swapped = jnp.where(first_half,
                            pltpu.roll(zc, LANES - HEAD_DIM // 2, 1),
                            pltpu.roll(zc, HEAD_DIM // 2, 1))
        outs.append(zc * cos + swapped * sin_signed)
    return jnp.concatenate(outs, axis=1)


def _rope_tile_t(zt, cos_t, sin_t):
    half = HEAD_DIM // 2
    outs = []
    for h in range(N_HEADS):
        x1 = zt[h * HEAD_DIM:h * HEAD_DIM + half, :]
        x2 = zt[h * HEAD_DIM + half:(h + 1) * HEAD_DIM, :]
        outs.append(x1 * cos_t - x2 * sin_t)
        outs.append(x2 * cos_t + x1 * sin_t)
    return jnp.concatenate(outs, axis=0)


def _layernorm_swish(acc, lng, lnb):
    mu = jnp.mean(acc, axis=-1, keepdims=True)
    cen = acc - mu
    var = jnp.mean(cen * cen, axis=-1, keepdims=True)
    return _silu(cen * lax.rsqrt(var + NORM_EPS) * lng + lnb)


def _inproj_kernel(*refs, tm, rc, fill_layers):
    (x_ref, mod_ref, ng_ref, wi_ref, cos_ref, sin_ref, hist_ref, dww_ref, dwb_ref, lng_ref, lnb_ref, pw2_ref,
     cg_ref) = refs[:13]
    wkv_ref, cost_ref, sint_ref = refs[13:16]
    refs = refs[16:]
    if len(refs) == 12:
        refs = refs[2:]
    q_ref, kt_ref, vt_ref, ga_ref, mc_ref, cs_ref, h_ref, xp_ref, gc_ref, sw_ref = refs
    a = ATTN_WIDTH
    off = HIST_PAD - CONV_HIST
    j = pl.program_id(1)

    @pl.when(j == 0)
    def _():
        hist = hist_ref[...]
        for c in range(N_SLABS):
            xp_ref[c, 0:off, :] = jnp.zeros((off, LANES), F32)
            xp_ref[c, off:HIST_PAD, :] = hist[:, _slab(c)]

    x = x_ref[...]
    shift = mod_ref[0:1, :]
    scale = mod_ref[1:2, :]
    xn = x * lax.rsqrt(jnp.mean(x * x, axis=-1, keepdims=True) + NORM_EPS)
    h_ref[...] = ((xn * ng_ref[...]) * (1.0 + scale) + shift).astype(BF16)

    def proj(g):
        return _dot(h_ref[...], wi_ref[:, g * a:(g + 1) * a])

    glu = proj(4) * jax.nn.sigmoid(proj(5))
    for c in range(N_SLABS):
        xp_ref[c, HIST_PAD:HIST_PAD + tm, :] = glu[:, _slab(c)]
    gc_ref[...] = _silu(proj(6))

    cos = cos_ref[...]
    sin = sin_ref[...]

    def emit_q():
        q_ref[...] = (_rope_tile(proj(0), cos, sin) * Q_SCALE).astype(BF16)

    def emit_ga():
        ga_ref[...] = _silu(proj(3))

    def put_t(dst_ref, val):
        if fill_layers is None:
            dst_ref[...] = val
        else:
            own, depth = fill_layers
            for o in range(depth):
                dst_ref[o] = val if o == own else jnp.zeros_like(val)

    def emit_kt():
        put_t(kt_ref, _rope_tile_t(_dot_nt(wkv_ref[0:a, :], h_ref[...]), cost_ref[...], sint_ref[...]))

    def emit_vt():
        put_t(vt_ref, _dot_nt(wkv_ref[a:2 * a, :], h_ref[...]))

    matmul_tasks = [emit_q, emit_kt, emit_vt, emit_ga]

    def conv_slab(c):
        bias = dwb_ref[:, _slab(c)]
        half = tm // 2
        for parity in range(2):
            acc = jnp.zeros((half, LANES), F32) + bias
            for tap in range(CONV_K):
                win = xp_ref[c, pl.ds(off + parity + tap, half, stride=2), :]
                acc = acc + win * dww_ref[tap:tap + 1, _slab(c)]
            sw_ref[c, pl.ds(parity, half, stride=2), :] = acc

    conv_tasks = [functools.partial(conv_slab, c) for c in range(N_SLABS)]

    for task in range(max(len(conv_tasks), len(matmul_tasks))):
        if task < len(conv_tasks):
            conv_tasks[task]()
        if task < len(matmul_tasks):
            matmul_tasks[task]()

    lng = lng_ref[...]
    lnb = lnb_ref[...]
    for r0 in range(0, tm, rc):
        dw = jnp.concatenate([sw_ref[c, r0:r0 + rc, :] for c in range(N_SLABS)], axis=1)
        y = _layernorm_swish(dw, lng, lnb)
        for c in range(N_SLABS):
            sw_ref[c, r0:r0 + rc, :] = y[:, _slab(c)]

    sw = jnp.concatenate([sw_ref[c] for c in range(N_SLABS)], axis=1).astype(BF16)
    co = _dot(sw, pw2_ref[...])
    con = co * lax.rsqrt(jnp.mean(co * co, axis=-1, keepdims=True) + NORM_EPS) * cg_ref[...]
    mc_ref[...] = (con * gc_ref[...]).astype(BF16)

    cs_ref[...] = jnp.concatenate([xp_ref[c, HIST_PAD + tm - CONV_HIST:HIST_PAD + tm, :] for c in range(N_SLABS)],
                                  axis=1)
    for c in range(N_SLABS):
        tail = xp_ref[c, tm:tm + HIST_PAD, :]
        xp_ref[c, 0:HIST_PAD, :] = tail


def _inproj(x, mod3, ng, wi_bf, tables, hist, dww, dwb, lng, lnb, pw2_bf, cg, *, tm, wkv_t, layer, depth, kv_prev):
    b, s, d = x.shape
    a = ATTN_WIDTH
    n_in = wi_bf.shape[1]
    rc = 64
    assert s % tm == 0 and tm % rc == 0 and tm % 2 == 0
    cos_t, sin_t, cos_tt, sin_tt = tables
    half = HEAD_DIM // 2
    row = lambda v: v.reshape(1, -1)
    const = lambda shape: pl.BlockSpec(shape, lambda bi, j: (0,) * len(shape))
    tile = lambda w: pl.BlockSpec((None, tm, w), lambda bi, j: (bi, j, 0))
    hist_spec = pl.BlockSpec((None, CONV_HIST, a), lambda bi, j: (bi, 0, 0))
    if kv_prev is None:
        tile_t = pl.BlockSpec((depth, None, a, tm), lambda bi, j: (0, bi, 0, j))
        fill_layers = (layer, depth)
    else:
        tile_t = pl.BlockSpec((None, None, a, tm), lambda bi, j: (layer, bi, 0, j))
        fill_layers = None
    in_specs = [
        tile(d),
        pl.BlockSpec((None, 3, d), lambda bi, j: (bi, 0, 0)),
        const((1, d)),
        const((d, n_in)),
        pl.BlockSpec((tm, LANES), lambda bi, j: (j, 0)),
        pl.BlockSpec((tm, LANES), lambda bi, j: (j, 0)),
        hist_spec,
        const((CONV_K, a)),
        const((1, a)),
        const((1, a)),
        const((1, a)),
        const((a, a)),
        const((1, a)),
        const((2 * a, d)),
        pl.BlockSpec((half, tm), lambda bi, j: (0, j)),
        pl.BlockSpec((half, tm), lambda bi, j: (0, j)),
    ]
    args = [x, mod3, row(ng), wi_bf, cos_t, sin_t, hist, dww, row(dwb), row(lng), row(lnb), pw2_bf, row(cg),
            wkv_t, cos_tt, sin_tt]
    aliases = {}
    if kv_prev is not None:
        aliases = {len(args): 1, len(args) + 1: 2}
        in_specs += [pl.BlockSpec(memory_space=pl.ANY)] * 2
        args += list(kv_prev)
    out_specs = [tile(a), tile_t, tile_t, tile(a), tile(a), hist_spec]
    out_shape = [
        jax.ShapeDtypeStruct((b, s, a), BF16),
        jax.ShapeDtypeStruct((depth, b, a, s), F32),
        jax.ShapeDtypeStruct((depth, b, a, s), F32),
        jax.ShapeDtypeStruct((b, s, a), F32),
        jax.ShapeDtypeStruct((b, s, a), BF16),
        jax.ShapeDtypeStruct((b, CONV_HIST, a), F32),
    ]
    return pl.pallas_call(
        functools.partial(_inproj_kernel, tm=tm, rc=rc, fill_layers=fill_layers),
        grid=(b, s // tm),
        in_specs=in_specs,
        out_specs=out_specs,
        out_shape=out_shape,
        input_output_aliases=aliases,
        scratch_shapes=[
            pltpu.VMEM((tm, d), BF16),
            pltpu.VMEM((N_SLABS, HIST_PAD + tm, LANES), F32),
            pltpu.VMEM((tm, a), F32),
            pltpu.VMEM((N_SLABS, tm, LANES), F32),
        ],
        compiler_params=_params("arbitrary", "arbitrary"),
        name="inproj_t",
    )(*args)


def _inproj_rows_kernel(x_ref, shift_ref, scale_ref, ng_ref, wi_ref, cos_ref, sin_ref, hist_ref, dww_ref, dwb_ref,
                        lng_ref, lnb_ref, pw2_ref, cg_ref,
                        q_ref, k_ref, v_ref, ga_ref, mc_ref, cs_ref, xp_ref, *, n_seq, t_new):
    a = ATTN_WIDTH
    off = HIST_PAD - CONV_HIST
    x = x_ref[...]
    xn = x * lax.rsqrt(jnp.mean(x * x, axis=-1, keepdims=True) + NORM_EPS)
    h = ((xn * ng_ref[...]) * (1.0 + scale_ref[...]) + shift_ref[...]).astype(BF16)

    def proj(g):
        return _dot(h, wi_ref[:, g * a:(g + 1) * a])

    cos = cos_ref[...]
    sin = sin_ref[...]
    q_ref[...] = (_rope_tile(proj(0), cos, sin) * Q_SCALE).astype(BF16)
    k_ref[...] = _rope_tile(proj(1), cos, sin)
    v_ref[...] = proj(2)
    ga_ref[...] = _silu(proj(3))
    glu = proj(4) * jax.nn.sigmoid(proj(5))
    gc = _silu(proj(6))

    taps = dww_ref[...]
    bias = dwb_ref[...]
    accs = []
    for g in range(n_seq):
        xp_ref[g, off:HIST_PAD, :] = hist_ref[g]
        xp_ref[g, HIST_PAD:HIST_PAD + t_new, :] = glu[g * t_new:(g + 1) * t_new, :]
        acc = jnp.zeros((t_new, a), F32) + bias
        for tap in range(CONV_K):
            acc = acc + xp_ref[g, off + tap:off + tap + t_new, :] * taps[tap:tap + 1, :]
        accs.append(acc)
        cs_ref[g] = xp_ref[g, HIST_PAD + t_new - CONV_HIST:HIST_PAD + t_new, :]
    sw = _layernorm_swish(jnp.concatenate(accs, axis=0), lng_ref[...], lnb_ref[...]).astype(BF16)
    co = _dot(sw, pw2_ref[...])
    con = co * lax.rsqrt(jnp.mean(co * co, axis=-1, keepdims=True) + NORM_EPS) * cg_ref[...]
    mc_ref[...] = (con * gc).astype(BF16)


def _inproj_rows(x2, shift_rows, scale_rows, ng, wi_bf, cos_rows, sin_rows, hist, dww, dwb, lng, lnb, pw2_bf, cg):
    rows, d = x2.shape
    n_seq = hist.shape[0]
    t_new = rows // n_seq
    a = ATTN_WIDTH
    row = lambda v: v.reshape(1, -1)
    args = [x2, shift_rows, scale_rows, row(ng), wi_bf, cos_rows, sin_rows, hist, dww, row(dwb), row(lng), row(lnb),
            pw2_bf, row(cg)]
    whole = lambda arr: pl.BlockSpec(arr.shape, lambda i: (0,) * arr.ndim)
    out_shape = [
        jax.ShapeDtypeStruct((rows, a), BF16),
        jax.ShapeDtypeStruct((rows, a), F32),
        jax.ShapeDtypeStruct((rows, a), F32),
        jax.ShapeDtypeStruct((rows, a), F32),
        jax.ShapeDtypeStruct((rows, a), BF16),
        jax.ShapeDtypeStruct((n_seq, CONV_HIST, a), F32),
    ]
    return pl.pallas_call(
        functools.partial(_inproj_rows_kernel, n_seq=n_seq, t_new=t_new),
        grid=(1,),
        in_specs=[whole(v) for v in args],
        out_specs=[whole(v) for v in out_shape],
        out_shape=out_shape,
        scratch_shapes=[pltpu.VMEM((n_seq, HIST_PAD + t_new, a), F32)],
        compiler_params=_params("arbitrary"),
        name="inproj_rows",
    )(*args)


def _mix_out(att, ga, mc_bf, x, gate, ag, wo_ref, fg, final):
    a = ATTN_WIDTH
    an = att * lax.rsqrt(jnp.mean(att * att, axis=-1, keepdims=True) + NORM_EPS) * ag
    ma = (an * ga).astype(BF16)
    y = _dot(ma, wo_ref[0:a, :]) + _dot(mc_bf, wo_ref[a:2 * a, :])
    xn = x + gate * y
    if final:
        xn = xn * lax.rsqrt(jnp.mean(xn * xn, axis=-1, keepdims=True) + NORM_EPS) * fg
    return xn


def _outproj_kernel(att_ref, ga_ref, mc_ref, x_ref, gate_ref, ag_ref, wo_ref, fg_ref, o_ref, *, final):
    o_ref[...] = _mix_out(att_ref[...], ga_ref[...], mc_ref[...], x_ref[...], gate_ref[...], ag_ref[...], wo_ref,
                          fg_ref[...], final)


def _outproj(att2, ga2, mc2, x2, gate_rows, ag, wo_bf, fg, *, final):
    rows, d = x2.shape
    a = ATTN_WIDTH
    args = [att2, ga2, mc2, x2, gate_rows, ag.reshape(1, a), wo_bf, fg.reshape(1, d)]
    whole = lambda arr: pl.BlockSpec(arr.shape, lambda i: (0,) * arr.ndim)
    return pl.pallas_call(
        functools.partial(_outproj_kernel, final=final),
        grid=(1,),
        in_specs=[whole(v) for v in args],
        out_specs=pl.BlockSpec((rows, d), lambda i: (0, 0)),
        out_shape=jax.ShapeDtypeStruct((rows, d), F32),
        compiler_params=_params("arbitrary"),
        name="outproj",
    )(*args)


def _head_lane_mask(h, rows, width):
    lane = lax.broadcasted_iota(jnp.int32, (rows, width), 1)
    return (lane // HEAD_DIM) == h


def _moba_kernel(q_ref, ktf_ref, vtf_ref, ga_ref, mc_ref, x_ref, mod_ref, ag_ref, wo_ref, fg_ref, o_ref,
                 kb_ref, vt_ref, km_ref, qm_ref, bias_ref, m_ref, l_ref, acc_ref, s0_ref, s1_ref, p_ref,
                 *, nb, final):
    blk = MOBA_BLOCK
    a = ATTN_WIDTH
    i = pl.program_id(1)

    @pl.when(i == 0)
    def _():
        for n in range(nb):
            vt_ref[n] = vtf_ref[:, n * blk:(n + 1) * blk].astype(BF16)
            kblk = ktf_ref[:, n * blk:(n + 1) * blk].T
            kb_ref[n * blk:(n + 1) * blk, :] = kblk.astype(BF16)
            mean_n = jnp.mean(kblk, axis=0, keepdims=True)
            for h in range(N_HEADS):
                km_ref[h * nb + n:h * nb + n + 1, :] = jnp.where(_head_lane_mask(h, 1, a), mean_n, 0.0)

    q = q_ref[...]
    for h in range(N_HEADS):
        pair = q[:, _slab(h // 2)]
        qm_ref[h] = jnp.where(_head_lane_mask(h % 2, blk, LANES), pair, jnp.zeros_like(pair))

    def key_block(n, hp):
        return kb_ref[pl.ds(pl.multiple_of(n * blk, blk), blk), _slab(hp)]

    def scores(n, dst_ref):
        for h in range(N_HEADS):
            dst_ref[h] = _dot_nt(key_block(n, h // 2), qm_ref[h])

    def values(n):
        return [_dot(vt_ref[n, h * HEAD_DIM:(h + 1) * HEAD_DIM, :], p_ref[h]) for h in range(N_HEADS)]

    scores(i, s0_ref)
    scores(0, s1_ref)

    gate_t = _dot_nt(km_ref[...].astype(BF16), q)
    n_iota = lax.broadcasted_iota(jnp.int32, (nb, blk), 0)
    past = n_iota < i
    for h in range(N_HEADS):
        g = jnp.where(past, gate_t[h * nb:(h + 1) * nb, :], -jnp.inf)
        beaten = jnp.zeros((nb, blk), jnp.int32)
        for m in range(nb):
            gm = g[m:m + 1, :]
            beaten = beaten + ((gm > g) | ((gm == g) & (m < n_iota))).astype(jnp.int32)
        sel = past & (beaten < MOBA_TOPK)
        bias_ref[h * nb:(h + 1) * nb, :] = jnp.where(sel, 0.0, NEG_BIG)

    key_i = lax.broadcasted_iota(jnp.int32, (blk, blk), 0)
    qry_i = lax.broadcasted_iota(jnp.int32, (blk, blk), 1)
    causal = key_i <= qry_i
    for h in range(N_HEADS):
        s = jnp.where(causal, s0_ref[h], NEG_BIG)
        mx = jnp.max(s, axis=0, keepdims=True)
        p = jnp.exp2(s - mx)
        m_ref[h:h + 1, :] = mx
        l_ref[h:h + 1, :] = jnp.sum(p, axis=0, keepdims=True)
        p_ref[h] = p.astype(BF16)
    for h, pv in enumerate(values(i)):
        acc_ref[h * HEAD_DIM:(h + 1) * HEAD_DIM, :] = pv

    def past_step(n, cur_ref, nxt_ref):
        for h in range(N_HEADS):
            nxt_ref[h] = _dot_nt(key_block(n + 1, h // 2), qm_ref[h])
            s = cur_ref[h]
            b = bias_ref[pl.ds(h * nb + n, 1), :]
            m_old = m_ref[h:h + 1, :]
            m_new = jnp.maximum(m_old, jnp.max(s, axis=0, keepdims=True) + b)
            alpha = jnp.exp2(m_old - m_new)
            p = jnp.exp2(s - (m_new - b))
            m_ref[h:h + 1, :] = m_new
            l_ref[h:h + 1, :] = alpha * l_ref[h:h + 1, :] + jnp.sum(p, axis=0, keepdims=True)
            sl = slice(h * HEAD_DIM, (h + 1) * HEAD_DIM)
            pv = _dot(vt_ref[n, sl, :], p.astype(BF16))
            acc_ref[sl, :] = alpha * acc_ref[sl, :] + pv

    def past_block(n, carry):
        @pl.when(n % 2 == 0)
        def _():
            past_step(n, s1_ref, s0_ref)

        @pl.when(n % 2 == 1)
        def _():
            past_step(n, s0_ref, s1_ref)

        return carry

    lax.fori_loop(0, i, past_block, 0)

    for h in range(N_HEADS):
        sl = slice(h * HEAD_DIM, (h + 1) * HEAD_DIM)
        acc_ref[sl, :] = acc_ref[sl, :] / l_ref[h:h + 1, :]
    att = acc_ref[...].T
    o_ref[...] = _mix_out(att, ga_ref[...], mc_ref[...], x_ref[...], mod_ref[2:3, :], ag_ref[...], wo_ref,
                          fg_ref[...], final)


def _moba_prompt(q, kt_all, vt_all, layer, ga, mc, x, mod3, ag, wo_bf, fg, *, final):
    b, s, d = x.shape
    a = ATTN_WIDTH
    blk = MOBA_BLOCK
    assert s % blk == 0 and s // blk >= MOBA_TOPK
    nb = s // blk
    assert nb % 8 == 0
    tile = lambda w: pl.BlockSpec((None, blk, w), lambda bi, i: (bi, i, 0))
    const = lambda shape: pl.BlockSpec(shape, lambda bi, i: (0,) * len(shape))
    return pl.pallas_call(
        functools.partial(_moba_kernel, nb=nb, final=final),
        grid=(b, nb),
        in_specs=[tile(a),
                  pl.BlockSpec((None, None, a, s), lambda bi, i: (layer, bi, 0, 0)),
                  pl.BlockSpec((None, None, a, s), lambda bi, i: (layer, bi, 0, 0)),
                  tile(a), tile(a), tile(d),
                  pl.BlockSpec((None, 3, d), lambda bi, i: (bi, 0, 0)),
                  const((1, a)), const((2 * a, d)), const((1, d))],
        out_specs=tile(d),
        out_shape=jax.ShapeDtypeStruct((b, s, d), F32),
        scratch_shapes=[
            pltpu.VMEM((s, a), BF16),
            pltpu.VMEM((nb, a, blk), BF16),
            pltpu.VMEM((N_HEADS * nb, a), F32),
            pltpu.VMEM((N_HEADS, blk, LANES), BF16),
            pltpu.VMEM((N_HEADS * nb, blk), F32),
            pltpu.VMEM((N_HEADS, blk), F32),
            pltpu.VMEM((N_HEADS, blk), F32),
            pltpu.VMEM((a, blk), F32),
            pltpu.VMEM((N_HEADS, blk, blk), F32),
            pltpu.VMEM((N_HEADS, blk, blk), F32),
            pltpu.VMEM((N_HEADS, blk, blk), BF16),
        ],
        compiler_params=_params("arbitrary", "arbitrary"),
        name="moba",
    )(q, kt_all, vt_all, ga, mc, x, mod3, ag.reshape(1, a), wo_bf, fg.reshape(1, d))


def _paged_kernel(pt_ref, ck_ref, cv_ref, q_ref, kn_ref, vn_ref, o_ref,
                  pages_ref, sem_ref, qbd_ref, sc_ref, gate_ref, bmax_ref, idx_ref, m_ref, so_ref, l_ref, acc_ref,
                  *, n_steps, page, t_new, layer):
    pg = PAGES_PER_STEP
    a = ATTN_WIDTH
    rows = N_HEADS * t_new
    blk = MOBA_BLOCK
    ppb = blk // page
    bps = pg // ppb
    steps = 2 * n_steps
    seq = pl.program_id(0)
    t = pl.program_id(1)
    n_seq = pl.num_programs(0)
    slot = t % PAGE_SLOTS
    ahead = PAGE_SLOTS - 1
    lane_n = lax.broadcasted_iota(jnp.int32, (rows, LANES), 1)

    def page_copy(cache_ref, sq, chunk, jp, dst_slot):
        src = cache_ref.at[layer, pt_ref[sq, chunk * pg + jp]]
        return pltpu.make_async_copy(src, pages_ref.at[dst_slot, jp], sem_ref.at[dst_slot])

    def start_step(sq, tt, dst_slot):
        @pl.when(tt < n_steps)
        def _():
            for jp in range(pg):
                page_copy(ck_ref, sq, tt, jp, dst_slot).start(priority=jp % 2)

        @pl.when(tt >= n_steps)
        def _():
            for jp in range(pg):
                page_copy(cv_ref, sq, tt - n_steps, jp, dst_slot).start(priority=jp % 2)

    @pl.when((seq == 0) & (t == 0))
    def _():
        for d in range(ahead):
            start_step(seq, d, d)

    later = t + ahead
    wrap = later >= steps

    @pl.when(jnp.logical_not(wrap & (seq == n_seq - 1)))
    def _():
        start_step(jnp.where(wrap, seq + 1, seq), jnp.where(wrap, later - steps, later), later % PAGE_SLOTS)

    for jp in range(pg):
        page_copy(ck_ref, 0, 0, jp, slot).wait()

    @pl.when(t == 0)
    def _():
        q = q_ref[...].astype(F32)
        for h in range(N_HEADS):
            qbd_ref[h * t_new:(h + 1) * t_new, :] = jnp.where(_head_lane_mask(h, t_new, a), q, 0.0)
        gate_ref[...] = jnp.full((rows, LANES), -jnp.inf, F32)
        bmax_ref[...] = jnp.full((rows, LANES), -jnp.inf, F32)

    @pl.when(t < n_steps)
    def _():
        qbd = qbd_ref[...].astype(BF16)
        gate = gate_ref[...]
        bmax = bmax_ref[...]
        for bi in range(bps):
            tot = None
            top = None
            for c in range(ppb):
                jp = bi * ppb + c
                s = _dot(qbd, pages_ref[slot, jp].astype(BF16))
                sc_ref[t * pg + jp] = s
                tot = s if tot is None else tot + s
                top = s if top is None else jnp.maximum(top, s)
            n = t * bps + bi
            gate = jnp.where(lane_n == n, jnp.sum(tot, axis=1, keepdims=True), gate)
            bmax = jnp.where(lane_n == n, jnp.max(top, axis=1, keepdims=True), bmax)
        gate_ref[...] = gate
        bmax_ref[...] = bmax

    @pl.when(t == n_steps - 1)
    def _():
        gate = gate_ref[...]
        sel = jnp.zeros((rows, LANES), jnp.bool_)
        for r in range(MOBA_TOPK):
            mx = jnp.max(gate, axis=1, keepdims=True)
            first = jnp.min(jnp.where(gate == mx, lane_n, LANES), axis=1, keepdims=True)
            pick = lane_n == first
            idx_ref[r] = jnp.broadcast_to(first, (rows, LANES))
            sel = sel | pick
            gate = jnp.where(pick, -jnp.inf, gate)
        kn = jnp.concatenate([kn_ref[...], jnp.zeros((LANES - t_new, a), F32)], axis=0).astype(BF16)
        s_own = _dot_nt(qbd_ref[...].astype(BF16), kn)
        row_t = lax.broadcasted_iota(jnp.int32, (rows, LANES), 0) % t_new
        s_own = jnp.where(lane_n <= row_t, s_own, NEG_BIG)
        mx = jnp.maximum(jnp.max(s_own, axis=1, keepdims=True),
                         jnp.max(jnp.where(sel, bmax_ref[...], NEG_BIG), axis=1, keepdims=True))
        p_own = jnp.exp2(s_own - mx)
        m_ref[...] = jnp.broadcast_to(mx, (rows, LANES))
        so_ref[...] = p_own
        l_ref[...] = p_own
        acc_ref[...] = jnp.zeros((rows, a), F32)

    @pl.when(t >= n_steps)
    def _():
        c0 = t - n_steps
        mx = m_ref[...]
        picks = [idx_ref[r] for r in range(MOBA_TOPK)]
        acc = acc_ref[...]
        lpart = l_ref[...]
        for jp in range(pg):
            n = c0 * bps + jp // ppb
            keep = picks[0] == n
            for r in range(1, MOBA_TOPK):
                keep = keep | (picks[r] == n)
            p = jnp.where(keep, jnp.exp2(sc_ref[c0 * pg + jp] - mx), 0.0)
            lpart = lpart + p
            acc = acc + _dot_nt(p.astype(BF16), pages_ref[slot, jp].astype(BF16))
        acc_ref[...] = acc
        l_ref[...] = lpart

    @pl.when(t == steps - 1)
    def _():
        vn = jnp.concatenate([vn_ref[...], jnp.zeros((LANES - t_new, a), F32)], axis=0).astype(BF16)
        lsum = jnp.sum(l_ref[...], axis=1, keepdims=True)
        o = (acc_ref[...] + _dot(so_ref[...].astype(BF16), vn)) / lsum
        att = jnp.zeros((t_new, a), F32)
        for h in range(N_HEADS):
            att = att + jnp.where(_head_lane_mask(h, t_new, a), o[h * t_new:(h + 1) * t_new, :], 0.0)
        o_ref[...] = att


def _paged_attention(q, kn, vn, cache_kt, cache_vt, page_table, layer):
    nseq, t_new, a = q.shape
    n_pages = page_table.shape[1]
    page = cache_kt.shape[3]
    pg = PAGES_PER_STEP
    assert n_pages % pg == 0 and MOBA_BLOCK % page == 0 and page == LANES and t_new == 8
    assert pg % (MOBA_BLOCK // page) == 0
    n_steps = n_pages // pg
    assert (2 * n_steps) % PAGE_SLOTS == 0 and 2 * n_steps >= PAGE_SLOTS
    n_keys = n_pages * page
    assert n_keys // MOBA_BLOCK <= LANES and n_keys // MOBA_BLOCK >= MOBA_TOPK
    rows = N_HEADS * t_new

    seq = lambda: pl.BlockSpec((None, t_new, a), lambda s, t, pt: (s, 0, 0))
    grid_spec = pltpu.PrefetchScalarGridSpec(
        num_scalar_prefetch=1,
        grid=(nseq, 2 * n_steps),
        in_specs=[pl.BlockSpec(memory_space=pl.ANY), pl.BlockSpec(memory_space=pl.ANY), seq(), seq(), seq()],
        out_specs=seq(),
        scratch_shapes=[
            pltpu.VMEM((PAGE_SLOTS, pg, a, page), F32),
            pltpu.SemaphoreType.DMA((PAGE_SLOTS,)),
            pltpu.VMEM((rows, a), F32),
            pltpu.VMEM((n_pages, rows, page), F32),
            pltpu.VMEM((rows, LANES), F32),
            pltpu.VMEM((rows, LANES), F32),
            pltpu.VMEM((MOBA_TOPK, rows, LANES), jnp.int32),
            pltpu.VMEM((rows, LANES), F32),
            pltpu.VMEM((rows, LANES), F32),
            pltpu.VMEM((rows, LANES), F32),
            pltpu.VMEM((rows, a), F32),
        ],
    )
    return pl.pallas_call(
        functools.partial(_paged_kernel, n_steps=n_steps, page=page, t_new=t_new, layer=layer),
        grid_spec=grid_spec,
        out_shape=jax.ShapeDtypeStruct((nseq, t_new, a), F32),
        compiler_params=_params("arbitrary", "arbitrary"),
        name="paged",
    )(page_table, cache_kt, cache_vt, q, kn, vn)


def _rope_tables(pos):
    half = HEAD_DIM // 2
    inv = ROPE_THETA ** (-jnp.arange(half, dtype=F32) * (2.0 / HEAD_DIM))
    ang = pos.astype(F32)[:, None] * inv[None, :]
    cos = jnp.cos(ang)
    sin = jnp.sin(ang)
    cos_t = jnp.tile(cos, (1, LANES // half))
    sin_t = jnp.tile(jnp.concatenate([-sin, sin], axis=1), (1, LANES // HEAD_DIM))
    return cos_t, sin_t, cos.T, sin.T


def _heads_last(xt, bp, sp):
    depth = xt.shape[0]
    return jnp.transpose(xt.reshape(depth, bp, N_HEADS, HEAD_DIM, sp), (0, 1, 4, 2, 3))


def kernel(x_prompt, x_sample, c_prompt, c_sample, cache_k, cache_v, state_conv, page_table, norm_gain, w_ada, b_ada,
           w_in, conv_dw_w, conv_dw_b, conv_ln_g, conv_ln_b, conv_pw2, attn_out_gain, conv_out_gain, w_out,
           final_gain):
    depth = w_in.shape[0]
    bp, sp, d = x_prompt.shape
    bs, ss, _ = x_sample.shape
    a = ATTN_WIDTH
    n_pool, page = cache_k.shape[1], cache_k.shape[2]
    past_len = page_table.shape[1] * page

    tab_p = _rope_tables(jnp.arange(sp, dtype=jnp.int32))
    tab_s = _rope_tables(past_len + jnp.arange(ss, dtype=jnp.int32))
    mod = _ada(jnp.concatenate([c_prompt, c_sample], axis=0), w_ada, b_ada)
    mod = mod.reshape(depth, bp + bs, 3, d)
    ckt = jnp.transpose(cache_k, (0, 1, 3, 4, 2)).reshape(depth, n_pool, a, page)
    cvt = jnp.transpose(cache_v, (0, 1, 3, 4, 2)).reshape(depth, n_pool, a, page)
    wi_bf = w_in.astype(BF16)
    wkv_t = jnp.transpose(w_in[:, :, a:3 * a], (0, 2, 1)).astype(BF16)
    pw2_bf = conv_pw2.astype(BF16)
    wo_bf = w_out.astype(BF16)
    zero_hist = jnp.zeros((bp, CONV_HIST, a), F32)

    cos_rows, sin_rows = (jnp.tile(tab, (bs, 1)) for tab in tab_s[:2])
    hp, hs = x_prompt, x_sample.reshape(bs * ss, d)
    kv_prev = None
    cp_l, ks_l, vs_l, cs_l = [], [], [], []
    for l in range(depth):
        final = l == depth - 1
        conv_prm = (conv_dw_w[l], conv_dw_b[l], conv_ln_g[l], conv_ln_b[l], pw2_bf[l], conv_out_gain[l])
        mod_p, mod_s = mod[l, :bp], mod[l, bp:]

        q, kt_all, vt_all, ga, mc, cst = _inproj(hp, mod_p, norm_gain[l], wi_bf[l], tab_p, zero_hist, *conv_prm,
                                                 tm=256, wkv_t=wkv_t[l], layer=l, depth=depth, kv_prev=kv_prev)
        kv_prev = (kt_all, vt_all)
        hp = _moba_prompt(q, kt_all, vt_all, l, ga, mc, hp, mod_p, attn_out_gain[l], wo_bf[l], final_gain,
                          final=final)
        cp_l.append(cst)

        per_row = lambda m: jnp.repeat(m, ss, axis=0)
        q, k, v, ga, mc, cst = _inproj_rows(hs, per_row(mod_s[:, 0]), per_row(mod_s[:, 1]), norm_gain[l], wi_bf[l],
                                            cos_rows, sin_rows, state_conv[l], *conv_prm)
        seqs = lambda m: m.reshape(bs, ss, a)
        att = _paged_attention(seqs(q), seqs(k), seqs(v), ckt, cvt, page_table, l)
        hs = _outproj(att.reshape(bs * ss, a), ga, mc, hs, per_row(mod_s[:, 2]), attn_out_gain[l], wo_bf[l],
                      final_gain, final=final)
        ks_l.append(k.reshape(bs, ss, N_HEADS, HEAD_DIM))
        vs_l.append(v.reshape(bs, ss, N_HEADS, HEAD_DIM))
        cs_l.append(cst)

    kt_all, vt_all = kv_prev
    return (hp, hs.reshape(bs, ss, d), _heads_last(kt_all, bp, sp), _heads_last(vt_all, bp, sp), jnp.stack(cp_l),
            jnp.stack(ks_l), jnp.stack(vs_l), jnp.stack(cs_l))
```

```python
import functools
import math

import jax
import jax.numpy as jnp
from jax import lax
from jax.experimental import pallas as pl
from jax.experimental.pallas import tpu as pltpu

N_HEADS = 8
HEAD_DIM = 64
ATTN_WIDTH = N_HEADS * HEAD_DIM
MOBA_BLOCK = 256
MOBA_TOPK = 3
CONV_K = 31
CONV_HIST = CONV_K - 1
ROPE_THETA = 10000.0
NORM_EPS = 1e-6

LANES = 128
N_SLABS = ATTN_WIDTH // LANES
HIST_PAD = 32
NEG_BIG = -1e30
VMEM_LIMIT_BYTES = 56 * 1024 * 1024
PAGES_PER_STEP = 16
PAGE_SLOTS = 8
Q_SCALE = math.log2(math.e) / math.sqrt(HEAD_DIM)

F32 = jnp.float32
BF16 = jnp.bfloat16


def _silu(x):
    return x * jax.nn.sigmoid(x)


def _dot(a, b):
    return jnp.dot(a, b, preferred_element_type=F32)


def _dot_nt(a, b):
    return lax.dot_general(a, b, (((1,), (1,)), ((), ())), preferred_element_type=F32)


def _params(*semantics):
    return pltpu.CompilerParams(dimension_semantics=semantics, vmem_limit_bytes=VMEM_LIMIT_BYTES)


def _slab(c):
    return slice(c * LANES, (c + 1) * LANES)


def _ada_kernel(c_ref, wa_ref, ba_ref, o_ref):
    s = _silu(c_ref[...]).astype(BF16)
    o_ref[...] = _dot(s, wa_ref[...].astype(BF16)) + ba_ref[...]


def _ada(c_all, w_ada, b_ada):
    depth, d, n3 = w_ada.shape
    rows = c_all.shape[0]
    tn = 512
    return pl.pallas_call(
        _ada_kernel,
        grid=(depth, n3 // tn),
        in_specs=[
            pl.BlockSpec((rows, d), lambda l, n: (0, 0)),
            pl.BlockSpec((None, d, tn), lambda l, n: (l, 0, n)),
            pl.BlockSpec((None, 1, tn), lambda l, n: (l, 0, n)),
        ],
        out_specs=pl.BlockSpec((None, rows, tn), lambda l, n: (l, 0, n)),
        out_shape=jax.ShapeDtypeStruct((depth, rows, n3), F32),
        compiler_params=_params("arbitrary", "arbitrary"),
        name="ada",
    )(c_all, w_ada, b_ada.reshape(depth, 1, n3))


def _rope_tile(z, cos, sin_signed):
    lane = lax.broadcasted_iota(jnp.int32, (z.shape[0], LANES), 1)
    first_half = (lane % HEAD_DIM) < (HEAD_DIM // 2)
    outs = []
    for c in range(z.shape[1] // LANES):
        zc = z[:, _slab(c)]
        swapped = jnp.where(first_half,
                            pltpu.roll(zc, LANES - HEAD_DIM // 2, 1),
                            pltpu.roll(zc, HEAD_DIM // 2, 1))
        outs.append(zc * cos + swapped * sin_signed)
    return jnp.concatenate(outs, axis=1)


def _rope_tile_t(zt, cos_t, sin_t):
    half = HEAD_DIM // 2
    outs = []
    for h in range(N_HEADS):
        x1 = zt[h * HEAD_DIM:h * HEAD_DIM + half, :]
        x2 = zt[h * HEAD_DIM + half:(h + 1) * HEAD_DIM, :]
        outs.append(x1 * cos_t - x2 * sin_t)
        outs.append(x2 * cos_t + x1 * sin_t)
    return jnp.concatenate(outs, axis=0)


def _layernorm_swish(acc, lng, lnb):
    mu = jnp.mean(acc, axis=-1, keepdims=True)
    cen = acc - mu
    var = jnp.mean(cen * cen, axis=-1, keepdims=True)
    return _silu(cen * lax.rsqrt(var + NORM_EPS) * lng + lnb)


def _inproj_kernel(*refs, tm, rc, fill_layers):
    (x_ref, mod_ref, ng_ref, wi_ref, cos_ref, sin_ref, hist_ref, dww_ref, dwb_ref, lng_ref, lnb_ref, pw2_ref,
     cg_ref) = refs[:13]
    wkv_ref, cost_ref, sint_ref = refs[13:16]
    refs = refs[16:]
    if len(refs) == 12:
        refs = refs[2:]
    q_ref, kt_ref, vt_ref, ga_ref, mc_ref, cs_ref, h_ref, xp_ref, gc_ref, sw_ref = refs
    a = ATTN_WIDTH
    off = HIST_PAD - CONV_HIST
    j = pl.program_id(1)

    @pl.when(j == 0)
    def _():
        hist = hist_ref[...]
        for c in range(N_SLABS):
            xp_ref[c, 0:off, :] = jnp.zeros((off, LANES), F32)
            xp_ref[c, off:HIST_PAD, :] = hist[:, _slab(c)]

    x = x_ref[...]
    shift = mod_ref[0:1, :]
    scale = mod_ref[1:2, :]
    xn = x * lax.rsqrt(jnp.mean(x * x, axis=-1, keepdims=True) + NORM_EPS)
    h_ref[...] = ((xn * ng_ref[...]) * (1.0 + scale) + shift).astype(BF16)

    def proj(g):
        return _dot(h_ref[...], wi_ref[:, g * a:(g + 1) * a])

    glu = proj(4) * jax.nn.sigmoid(proj(5))
    for c in range(N_SLABS):
        xp_ref[c, HIST_PAD:HIST_PAD + tm, :] = glu[:, _slab(c)]
    gc_ref[...] = _silu(proj(6))

    cos = cos_ref[...]
    sin = sin_ref[...]

    def emit_q():
        q_ref[...] = (_rope_tile(proj(0), cos, sin) * Q_SCALE).astype(BF16)

    def emit_ga():
        ga_ref[...] = _silu(proj(3))

    def put_t(dst_ref, val):
        if fill_layers is None:
            dst_ref[...] = val
        else:
            own, depth = fill_layers
            for o in range(depth):
                dst_ref[o] = val if o == own else jnp.zeros_like(val)

    def emit_kt():
        put_t(kt_ref, _rope_tile_t(_dot_nt(wkv_ref[0:a, :], h_ref[...]), cost_ref[...], sint_ref[...]))

    def emit_vt():
        put_t(vt_ref, _dot_nt(wkv_ref[a:2 * a, :], h_ref[...]))

    matmul_tasks = [emit_q, emit_kt, emit_vt, emit_ga]

    def conv_slab(c):
        bias = dwb_ref[:, _slab(c)]
        half = tm // 2
        for parity in range(2):
            acc = jnp.zeros((half, LANES), F32) + bias
            for tap in range(CONV_K):
                win = xp_ref[c, pl.ds(off + parity + tap, half, stride=2), :]
                acc = acc + win * dww_ref[tap:tap + 1, _slab(c)]
            sw_ref[c, pl.ds(parity, half, stride=2), :] = acc

    conv_tasks = [functools.partial(conv_slab, c) for c in range(N_SLABS)]

    for task in range(max(len(conv_tasks), len(matmul_tasks))):
        if task < len(conv_tasks):
            conv_tasks[task]()
        if task < len(matmul_tasks):
            matmul_tasks[task]()

    lng = lng_ref[...]
    lnb = lnb_ref[...]
    for r0 in range(0, tm, rc):
        dw = jnp.concatenate([sw_ref[c, r0:r0 + rc, :] for c in range(N_SLABS)], axis=1)
        y = _layernorm_swish(dw, lng, lnb)
        for c in range(N_SLABS):
            sw_ref[c, r0:r0 + rc, :] = y[:, _slab(c)]

    sw = jnp.concatenate([sw_ref[c] for c in range(N_SLABS)], axis=1).astype(BF16)
    co = _dot(sw, pw2_ref[...])
    con = co * lax.rsqrt(jnp.mean(co * co, axis=-1, keepdims=True) + NORM_EPS) * cg_ref[...]
    mc_ref[...] = (con * gc_ref[...]).astype(BF16)

    cs_ref[...] = jnp.concatenate([xp_ref[c, HIST_PAD + tm - CONV_HIST:HIST_PAD + tm, :] for c in range(N_SLABS)],
                                  axis=1)
    for c in range(N_SLABS):
        tail = xp_ref[c, tm:tm + HIST_PAD, :]
        xp_ref[c, 0:HIST_PAD, :] = tail


def _inproj(x, mod3, ng, wi_bf, tables, hist, dww, dwb, lng, lnb, pw2_bf, cg, *, tm, wkv_t, layer, depth, kv_prev):
    b, s, d = x.shape
    a = ATTN_WIDTH
    n_in = wi_bf.shape[1]
    rc = 64
    assert s % tm == 0 and tm % rc == 0 and tm % 2 == 0
    cos_t, sin_t, cos_tt, sin_tt = tables
    half = HEAD_DIM // 2
    row = lambda v: v.reshape(1, -1)
    const = lambda shape: pl.BlockSpec(shape, lambda bi, j: (0,) * len(shape))
    tile = lambda w: pl.BlockSpec((None, tm, w), lambda bi, j: (bi, j, 0))
    hist_spec = pl.BlockSpec((None, CONV_HIST, a), lambda bi, j: (bi, 0, 0))
    if kv_prev is None:
        tile_t = pl.BlockSpec((depth, None, a, tm), lambda bi, j: (0, bi, 0, j))
        fill_layers = (layer, depth)
    else:
        tile_t = pl.BlockSpec((None, None, a, tm), lambda bi, j: (layer, bi, 0, j))
        fill_layers = None
    in_specs = [
        tile(d),
        pl.BlockSpec((None, 3, d), lambda bi, j: (bi, 0, 0)),
        const((1, d)),
        const((d, n_in)),
        pl.BlockSpec((tm, LANES), lambda bi, j: (j, 0)),
        pl.BlockSpec((tm, LANES), lambda bi, j: (j, 0)),
        hist_spec,
        const((CONV_K, a)),
        const((1, a)),
        const((1, a)),
        const((1, a)),
        const((a, a)),
        const((1, a)),
        const((2 * a, d)),
        pl.BlockSpec((half, tm), lambda bi, j: (0, j)),
        pl.BlockSpec((half, tm), lambda bi, j: (0, j)),
    ]
    args = [x, mod3, row(ng), wi_bf, cos_t, sin_t, hist, dww, row(dwb), row(lng), row(lnb), pw2_bf, row(cg),
            wkv_t, cos_tt, sin_tt]
    aliases = {}
    if kv_prev is not None:
        aliases = {len(args): 1, len(args) + 1: 2}
        in_specs += [pl.BlockSpec(memory_space=pl.ANY)] * 2
        args += list(kv_prev)
    out_specs = [tile(a), tile_t, tile_t, tile(a), tile(a), hist_spec]
    out_shape = [
        jax.ShapeDtypeStruct((b, s, a), BF16),
        jax.ShapeDtypeStruct((depth, b, a, s), F32),
        jax.ShapeDtypeStruct((depth, b, a, s), F32),
        jax.ShapeDtypeStruct((b, s, a), F32),
        jax.ShapeDtypeStruct((b, s, a), BF16),
        jax.ShapeDtypeStruct((b, CONV_HIST, a), F32),
    ]
    return pl.pallas_call(
        functools.partial(_inproj_kernel, tm=tm, rc=rc, fill_layers=fill_layers),
        grid=(b, s // tm),
        in_specs=in_specs,
        out_specs=out_specs,
        out_shape=out_shape,
        input_output_aliases=aliases,
        scratch_shapes=[
            pltpu.VMEM((tm, d), BF16),
            pltpu.VMEM((N_SLABS, HIST_PAD + tm, LANES), F32),
            pltpu.VMEM((tm, a), F32),
            pltpu.VMEM((N_SLABS, tm, LANES), F32),
        ],
        compiler_params=_params("arbitrary", "arbitrary"),
        name="inproj_t",
    )(*args)


def _inproj_rows_kernel(x_ref, shift_ref, scale_ref, ng_ref, wi_ref, cos_ref, sin_ref, hist_ref, dww_ref, dwb_ref,
                        lng_ref, lnb_ref, pw2_ref, cg_ref,
                        q_ref, k_ref, v_ref, ga_ref, mc_ref, cs_ref, xp_ref, *, n_seq, t_new):
    a = ATTN_WIDTH
    off = HIST_PAD - CONV_HIST
    x = x_ref[...]
    xn = x * lax.rsqrt(jnp.mean(x * x, axis=-1, keepdims=True) + NORM_EPS)
    h = ((xn * ng_ref[...]) * (1.0 + scale_ref[...]) + shift_ref[...]).astype(BF16)

    def proj(g):
        return _dot(h, wi_ref[:, g * a:(g + 1) * a])

    cos = cos_ref[...]
    sin = sin_ref[...]
    q_ref[...] = (_rope_tile(proj(0), cos, sin) * Q_SCALE).astype(BF16)
    k_ref[...] = _rope_tile(proj(1), cos, sin)
    v_ref[...] = proj(2)
    ga_ref[...] = _silu(proj(3))
    glu = proj(4) * jax.nn.sigmoid(proj(5))
    gc = _silu(proj(6))

    taps = dww_ref[...]
    bias = dwb_ref[...]
    accs = []
    for g in range(n_seq):
        xp_ref[g, off:HIST_PAD, :] = hist_ref[g]
        xp_ref[g, HIST_PAD:HIST_PAD + t_new, :] = glu[g * t_new:(g + 1) * t_new, :]
        acc = jnp.zeros((t_new, a), F32) + bias
        for tap in range(CONV_K):
            acc = acc + xp_ref[g, off + tap:off + tap + t_new, :] * taps[tap:tap + 1, :]
        accs.append(acc)
        cs_ref[g] = xp_ref[g, HIST_PAD + t_new - CONV_HIST:HIST_PAD + t_new, :]
    sw = _layernorm_swish(jnp.concatenate(accs, axis=0), lng_ref[...], lnb_ref[...]).astype(BF16)
    co = _dot(sw, pw2_ref[...])
    con = co * lax.rsqrt(jnp.mean(co * co, axis=-1, keepdims=True) + NORM_EPS) * cg_ref[...]
    mc_ref[...] = (con * gc).astype(BF16)


def _inproj_rows(x2, shift_rows, scale_rows, ng, wi_bf, cos_rows, sin_rows, hist, dww, dwb, lng, lnb, pw2_bf, cg):
    rows, d = x2.shape
    n_seq = hist.shape[0]
    t_new = rows // n_seq
    a = ATTN_WIDTH
    row = lambda v: v.reshape(1, -1)
    args = [x2, shift_rows, scale_rows, row(ng), wi_bf, cos_rows, sin_rows, hist, dww, row(dwb), row(lng), row(lnb),
            pw2_bf, row(cg)]
    whole = lambda arr: pl.BlockSpec(arr.shape, lambda i: (0,) * arr.ndim)
    out_shape = [
        jax.ShapeDtypeStruct((rows, a), BF16),
        jax.ShapeDtypeStruct((rows, a), F32),
        jax.ShapeDtypeStruct((rows, a), F32),
        jax.ShapeDtypeStruct((rows, a), F32),
        jax.ShapeDtypeStruct((rows, a), BF16),
        jax.ShapeDtypeStruct((n_seq, CONV_HIST, a), F32),
    ]
    return pl.pallas_call(
        functools.partial(_inproj_rows_kernel, n_seq=n_seq, t_new=t_new),
        grid=(1,),
        in_specs=[whole(v) for v in args],
        out_specs=[whole(v) for v in out_shape],
        out_shape=out_shape,
        scratch_shapes=[pltpu.VMEM((n_seq, HIST_PAD + t_new, a), F32)],
        compiler_params=_params("arbitrary"),
        name="inproj_rows",
    )(*args)


def _mix_out(att, ga, mc_bf, x, gate, ag, wo_ref, fg, final):
    a = ATTN_WIDTH
    an = att * lax.rsqrt(jnp.mean(att * att, axis=-1, keepdims=True) + NORM_EPS) * ag
    ma = (an * ga).astype(BF16)
    y = _dot(ma, wo_ref[0:a, :]) + _dot(mc_bf, wo_ref[a:2 * a, :])
    xn = x + gate * y
    if final:
        xn = xn * lax.rsqrt(jnp.mean(xn * xn, axis=-1, keepdims=True) + NORM_EPS) * fg
    return xn


def _outproj_kernel(att_ref, ga_ref, mc_ref, x_ref, gate_ref, ag_ref, wo_ref, fg_ref, o_ref, *, final):
    o_ref[...] = _mix_out(att_ref[...], ga_ref[...], mc_ref[...], x_ref[...], gate_ref[...], ag_ref[...], wo_ref,
                          fg_ref[...], final)


def _outproj(att2, ga2, mc2, x2, gate_rows, ag, wo_bf, fg, *, final):
    rows, d = x2.shape
    a = ATTN_WIDTH
    args = [att2, ga2, mc2, x2, gate_rows, ag.reshape(1, a), wo_bf, fg.reshape(1, d)]
    whole = lambda arr: pl.BlockSpec(arr.shape, lambda i: (0,) * arr.ndim)
    return pl.pallas_call(
        functools.partial(_outproj_kernel, final=final),
        grid=(1,),
        in_specs=[whole(v) for v in args],
        out_specs=pl.BlockSpec((rows, d), lambda i: (0, 0)),
        out_shape=jax.ShapeDtypeStruct((rows, d), F32),
        compiler_params=_params("arbitrary"),
        name="outproj",
    )(*args)


def _head_lane_mask(h, rows, width):
    lane = lax.broadcasted_iota(jnp.int32, (rows, width), 1)
    return (lane // HEAD_DIM) == h


def _moba_kernel(q_ref, ktf_ref, vtf_ref, ga_ref, mc_ref, x_ref, mod_ref, ag_ref, wo_ref, fg_ref, o_ref,
                 kb_ref, vt_ref, km_ref, qm_ref, bias_ref, m_ref, l_ref, acc_ref, s0_ref, s1_ref, p_ref,
                 *, nb, final):
    blk = MOBA_BLOCK
    a = ATTN_WIDTH
    i = pl.program_id(1)

    @pl.when(i == 0)
    def _():
        for n in range(nb):
            vt_ref[n] = vtf_ref[:, n * blk:(n + 1) * blk].astype(BF16)
            kblk = ktf_ref[:, n * blk:(n + 1) * blk].T
            kb_ref[n * blk:(n + 1) * blk, :] = kblk.astype(BF16)
            mean_n = jnp.mean(kblk, axis=0, keepdims=True)
            for h in range(N_HEADS):
                km_ref[h * nb + n:h * nb + n + 1, :] = jnp.where(_head_lane_mask(h, 1, a), mean_n, 0.0)

    q = q_ref[...]
    for h in range(N_HEADS):
        pair = q[:, _slab(h // 2)]
        qm_ref[h] = jnp.where(_head_lane_mask(h % 2, blk, LANES), pair, jnp.zeros_like(pair))

    def key_block(n, hp):
        return kb_ref[pl.ds(pl.multiple_of(n * blk, blk), blk), _slab(hp)]

    def scores(n, dst_ref):
        for h in range(N_HEADS):
            dst_ref[h] = _dot_nt(key_block(n, h // 2), qm_ref[h])

    def values(n):
        return [_dot(vt_ref[n, h * HEAD_DIM:(h + 1) * HEAD_DIM, :], p_ref[h]) for h in range(N_HEADS)]

    scores(i, s0_ref)
    scores(0, s1_ref)

    gate_t = _dot_nt(km_ref[...].astype(BF16), q)
    n_iota = lax.broadcasted_iota(jnp.int32, (nb, blk), 0)
    past = n_iota < i
    for h in range(N_HEADS):
        g = jnp.where(past, gate_t[h * nb:(h + 1) * nb, :], -jnp.inf)
        beaten = jnp.zeros((nb, blk), jnp.int32)
        for m in range(nb):
            gm = g[m:m + 1, :]
            beaten = beaten + ((gm > g) | ((gm == g) & (m < n_iota))).astype(jnp.int32)
        sel = past & (beaten < MOBA_TOPK)
        bias_ref[h * nb:(h + 1) * nb, :] = jnp.where(sel, 0.0, NEG_BIG)

    key_i = lax.broadcasted_iota(jnp.int32, (blk, blk), 0)
    qry_i = lax.broadcasted_iota(jnp.int32, (blk, blk), 1)
    causal = key_i <= qry_i
    for h in range(N_HEADS):
        s = jnp.where(causal, s0_ref[h], NEG_BIG)
        mx = jnp.max(s, axis=0, keepdims=True)
        p = jnp.exp2(s - mx)
        m_ref[h:h + 1, :] = mx
        l_ref[h:h + 1, :] = jnp.sum(p, axis=0, keepdims=True)
        p_ref[h] = p.astype(BF16)
    for h, pv in enumerate(values(i)):
        acc_ref[h * HEAD_DIM:(h + 1) * HEAD_DIM, :] = pv

    def past_step(n, cur_ref, nxt_ref):
        for h in range(N_HEADS):
            nxt_ref[h] = _dot_nt(key_block(n + 1, h // 2), qm_ref[h])
            s = cur_ref[h]
            b = bias_ref[pl.ds(h * nb + n, 1), :]
            m_old = m_ref[h:h + 1, :]
            m_new = jnp.maximum(m_old, jnp.max(s, axis=0, keepdims=True) + b)
            alpha = jnp.exp2(m_old - m_new)
            p = jnp.exp2(s - (m_new - b))
            m_ref[h:h + 1, :] = m_new
            l_ref[h:h + 1, :] = alpha * l_ref[h:h + 1, :] + jnp.sum(p, axis=0, keepdims=True)
            sl = slice(h * HEAD_DIM, (h + 1) * HEAD_DIM)
            pv = _dot(vt_ref[n, sl, :], p.astype(BF16))
            acc_ref[sl, :] = alpha * acc_ref[sl, :] + pv

    def past_block(n, carry):
        @pl.when(n % 2 == 0)
        def _():
            past_step(n, s1_ref, s0_ref)

        @pl.when(n % 2 == 1)
        def _():
            past_step(n, s0_ref, s1_ref)

        return carry

    lax.fori_loop(0, i, past_block, 0)

    for h in range(N_HEADS):
        sl = slice(h * HEAD_DIM, (h + 1) * HEAD_DIM)
        acc_ref[sl, :] = acc_ref[sl, :] / l_ref[h:h + 1, :]
    att = acc_ref[...].T
    o_ref[...] = _mix_out(att, ga_ref[...], mc_ref[...], x_ref[...], mod_ref[2:3, :], ag_ref[...], wo_ref,
                          fg_ref[...], final)


def _moba_prompt(q, kt_all, vt_all, layer, ga, mc, x, mod3, ag, wo_bf, fg, *, final):
    b, s, d = x.shape
    a = ATTN_WIDTH
    blk = MOBA_BLOCK
    assert s % blk == 0 and s // blk >= MOBA_TOPK
    nb = s // blk
    assert nb % 8 == 0
    tile = lambda w: pl.BlockSpec((None, blk, w), lambda bi, i: (bi, i, 0))
    const = lambda shape: pl.BlockSpec(shape, lambda bi, i: (0,) * len(shape))
    return pl.pallas_call(
        functools.partial(_moba_kernel, nb=nb, final=final),
        grid=(b, nb),
        in_specs=[tile(a),
                  pl.BlockSpec((None, None, a, s), lambda bi, i: (layer, bi, 0, 0)),
                  pl.BlockSpec((None, None, a, s), lambda bi, i: (layer, bi, 0, 0)),
                  tile(a), tile(a), tile(d),
                  pl.BlockSpec((None, 3, d), lambda bi, i: (bi, 0, 0)),
                  const((1, a)), const((2 * a, d)), const((1, d))],
        out_specs=tile(d),
        out_shape=jax.ShapeDtypeStruct((b, s, d), F32),
        scratch_shapes=[
            pltpu.VMEM((s, a), BF16),
            pltpu.VMEM((nb, a, blk), BF16),
            pltpu.VMEM((N_HEADS * nb, a), F32),
            pltpu.VMEM((N_HEADS, blk, LANES), BF16),
            pltpu.VMEM((N_HEADS * nb, blk), F32),
            pltpu.VMEM((N_HEADS, blk), F32),
            pltpu.VMEM((N_HEADS, blk), F32),
            pltpu.VMEM((a, blk), F32),
            pltpu.VMEM((N_HEADS, blk, blk), F32),
            pltpu.VMEM((N_HEADS, blk, blk), F32),
            pltpu.VMEM((N_HEADS, blk, blk), BF16),
        ],
        compiler_params=_params("arbitrary", "arbitrary"),
        name="moba",
    )(q, kt_all, vt_all, ga, mc, x, mod3, ag.reshape(1, a), wo_bf, fg.reshape(1, d))


def _paged_kernel(pt_ref, ck_ref, cv_ref, q_ref, kn_ref, vn_ref, o_ref,
                  pages_ref, sem_ref, qbd_ref, sc_ref, gate_ref, bmax_ref, idx_ref, m_ref, so_ref, l_ref, acc_ref,
                  *, n_steps, page, t_new, layer):
    pg = PAGES_PER_STEP
    a = ATTN_WIDTH
    rows = N_HEADS * t_new
    blk = MOBA_BLOCK
    ppb = blk // page
    bps = pg // ppb
    steps = 2 * n_steps
    seq = pl.program_id(0)
    t = pl.program_id(1)
    n_seq = pl.num_programs(0)
    slot = t % PAGE_SLOTS
    ahead = PAGE_SLOTS - 1
    lane_n = lax.broadcasted_iota(jnp.int32, (rows, LANES), 1)

    def page_copy(cache_ref, sq, chunk, jp, dst_slot):
        src = cache_ref.at[layer, pt_ref[sq, chunk * pg + jp]]
        return pltpu.make_async_copy(src, pages_ref.at[dst_slot, jp], sem_ref.at[dst_slot])

    def start_step(sq, tt, dst_slot):
        @pl.when(tt < n_steps)
        def _():
            for jp in range(pg):
                page_copy(ck_ref, sq, tt, jp, dst_slot).start(priority=jp % 2)

        @pl.when(tt >= n_steps)
        def _():
            for jp in range(pg):
                page_copy(cv_ref, sq, tt - n_steps, jp, dst_slot).start(priority=jp % 2)

    @pl.when((seq == 0) & (t == 0))
    def _():
        for d in range(ahead):
            start_step(seq, d, d)

    later = t + ahead
    wrap = later >= steps

    @pl.when(jnp.logical_not(wrap & (seq == n_seq - 1)))
    def _():
        start_step(jnp.where(wrap, seq + 1, seq), jnp.where(wrap, later - steps, later), later % PAGE_SLOTS)

    for jp in range(pg):
        page_copy(ck_ref, 0, 0, jp, slot).wait()

    @pl.when(t == 0)
    def _():
        q = q_ref[...].astype(F32)
        for h in range(N_HEADS):
            qbd_ref[h * t_new:(h + 1) * t_new, :] = jnp.where(_head_lane_mask(h, t_new, a), q, 0.0)
        gate_ref[...] = jnp.full((rows, LANES), -jnp.inf, F32)
        bmax_ref[...] = jnp.full((rows, LANES), -jnp.inf, F32)

    @pl.when(t < n_steps)
    def _():
        qbd = qbd_ref[...].astype(BF16)
        gate = gate_ref[...]
        bmax = bmax_ref[...]
        for bi in range(bps):
            tot = None
            top = None
            for c in range(ppb):
                jp = bi * ppb + c
                s = _dot(qbd, pages_ref[slot, jp].astype(BF16))
                sc_ref[t * pg + jp] = s
                tot = s if tot is None else tot + s
                top = s if top is None else jnp.maximum(top, s)
            n = t * bps + bi
            gate = jnp.where(lane_n == n, jnp.sum(tot, axis=1, keepdims=True), gate)
            bmax = jnp.where(lane_n == n, jnp.max(top, axis=1, keepdims=True), bmax)
        gate_ref[...] = gate
        bmax_ref[...] = bmax

    @pl.when(t == n_steps - 1)
    def _():
        gate = gate_ref[...]
        sel = jnp.zeros((rows, LANES), jnp.bool_)
        for r in range(MOBA_TOPK):
            mx = jnp.max(gate, axis=1, keepdims=True)
            first = jnp.min(jnp.where(gate == mx, lane_n, LANES), axis=1, keepdims=True)
            pick = lane_n == first
            idx_ref[r] = jnp.broadcast_to(first, (rows, LANES))
            sel = sel | pick
            gate = jnp.where(pick, -jnp.inf, gate)
        kn = jnp.concatenate([kn_ref[...], jnp.zeros((LANES - t_new, a), F32)], axis=0).astype(BF16)
        s_own = _dot_nt(qbd_ref[...].astype(BF16), kn)
        row_t = lax.broadcasted_iota(jnp.int32, (rows, LANES), 0) % t_new
        s_own = jnp.where(lane_n <= row_t, s_own, NEG_BIG)
        mx = jnp.maximum(jnp.max(s_own, axis=1, keepdims=True),
                         jnp.max(jnp.where(sel, bmax_ref[...], NEG_BIG), axis=1, keepdims=True))
        p_own = jnp.exp2(s_own - mx)
        m_ref[...] = jnp.broadcast_to(mx, (rows, LANES))
        so_ref[...] = p_own
        l_ref[...] = p_own
        acc_ref[...] = jnp.zeros((rows, a), F32)

    @pl.when(t >= n_steps)
    def _():
        c0 = t - n_steps
        mx = m_ref[...]
        picks = [idx_ref[r] for r in range(MOBA_TOPK)]
        acc = acc_ref[...]
        lpart = l_ref[...]
        for jp in range(pg):
            n = c0 * bps + jp // ppb
            keep = picks[0] == n
            for r in range(1, MOBA_TOPK):
                keep = keep | (picks[r] == n)
            p = jnp.where(keep, jnp.exp2(sc_ref[c0 * pg + jp] - mx), 0.0)
            lpart = lpart + p
            acc = acc + _dot_nt(p.astype(BF16), pages_ref[slot, jp].astype(BF16))
        acc_ref[...] = acc
        l_ref[...] = lpart

    @pl.when(t == steps - 1)
    def _():
        vn = jnp.concatenate([vn_ref[...], jnp.zeros((LANES - t_new, a), F32)], axis=0).astype(BF16)
        lsum = jnp.sum(l_ref[...], axis=1, keepdims=True)
        o = (acc_ref[...] + _dot(so_ref[...].astype(BF16), vn)) / lsum
        att = jnp.zeros((t_new, a), F32)
        for h in range(N_HEADS):
            att = att + jnp.where(_head_lane_mask(h, t_new, a), o[h * t_new:(h + 1) * t_new, :], 0.0)
        o_ref[...] = att


def _paged_attention(q, kn, vn, cache_kt, cache_vt, page_table, layer):
    nseq, t_new, a = q.shape
    n_pages = page_table.shape[1]
    page = cache_kt.shape[3]
    pg = PAGES_PER_STEP
    assert n_pages % pg == 0 and MOBA_BLOCK % page == 0 and page == LANES and t_new == 8
    assert pg % (MOBA_BLOCK // page) == 0
    n_steps = n_pages // pg
    assert (2 * n_steps) % PAGE_SLOTS == 0 and 2 * n_steps >= PAGE_SLOTS
    n_keys = n_pages * page
    assert n_keys // MOBA_BLOCK <= LANES and n_keys // MOBA_BLOCK >= MOBA_TOPK
    rows = N_HEADS * t_new

    seq = lambda: pl.BlockSpec((None, t_new, a), lambda s, t, pt: (s, 0, 0))
    grid_spec = pltpu.PrefetchScalarGridSpec(
        num_scalar_prefetch=1,
        grid=(nseq, 2 * n_steps),
        in_specs=[pl.BlockSpec(memory_space=pl.ANY), pl.BlockSpec(memory_space=pl.ANY), seq(), seq(), seq()],
        out_specs=seq(),
        scratch_shapes=[
            pltpu.VMEM((PAGE_SLOTS, pg, a, page), F32),
            pltpu.SemaphoreType.DMA((PAGE_SLOTS,)),
            pltpu.VMEM((rows, a), F32),
            pltpu.VMEM((n_pages, rows, page), F32),
            pltpu.VMEM((rows, LANES), F32),
            pltpu.VMEM((rows, LANES), F32),
            pltpu.VMEM((MOBA_TOPK, rows, LANES), jnp.int32),
            pltpu.VMEM((rows, LANES), F32),
            pltpu.VMEM((rows, LANES), F32),
            pltpu.VMEM((rows, LANES), F32),
            pltpu.VMEM((rows, a), F32),
        ],
    )
    return pl.pallas_call(
        functools.partial(_paged_kernel, n_steps=n_steps, page=page, t_new=t_new, layer=layer),
        grid_spec=grid_spec,
        out_shape=jax.ShapeDtypeStruct((nseq, t_new, a), F32),
        compiler_params=_params("arbitrary", "arbitrary"),
        name="paged",
    )(page_table, cache_kt, cache_vt, q, kn, vn)


def _rope_tables(pos):
    half = HEAD_DIM // 2
    inv = ROPE_THETA ** (-jnp.arange(half, dtype=F32) * (2.0 / HEAD_DIM))
    ang = pos.astype(F32)[:, None] * inv[None, :]
    cos = jnp.cos(ang)
    sin = jnp.sin(ang)
    cos_t = jnp.tile(cos, (1, LANES // half))
    sin_t = jnp.tile(jnp.concatenate([-sin, sin], axis=1), (1, LANES // HEAD_DIM))
    return cos_t, sin_t, cos.T, sin.T


def _heads_last(xt, bp, sp):
    depth = xt.shape[0]
    return jnp.transpose(xt.reshape(depth, bp, N_HEADS, HEAD_DIM, sp), (0, 1, 4, 2, 3))


def kernel(x_prompt, x_sample, c_prompt, c_sample, cache_k, cache_v, state_conv, page_table, norm_gain, w_ada, b_ada,
           w_in, conv_dw_w, conv_dw_b, conv_ln_g, conv_ln_b, conv_pw2, attn_out_gain, conv_out_gain, w_out,
           final_gain):
    depth = w_in.shape[0]
    bp, sp, d = x_prompt.shape
    bs, ss, _ = x_sample.shape
    a = ATTN_WIDTH
    n_pool, page = cache_k.shape[1], cache_k.shape[2]
    past_len = page_table.shape[1] * page

    tab_p = _rope_tables(jnp.arange(sp, dtype=jnp.int32))
    tab_s = _rope_tables(past_len + jnp.arange(ss, dtype=jnp.int32))
    mod = _ada(jnp.concatenate([c_prompt, c_sample], axis=0), w_ada, b_ada)
    mod = mod.reshape(depth, bp + bs, 3, d)
    ckt = jnp.transpose(cache_k, (0, 1, 3, 4, 2)).reshape(depth, n_pool, a, page)
    cvt = jnp.transpose(cache_v, (0, 1, 3, 4, 2)).reshape(depth, n_pool, a, page)
    wi_bf = w_in.astype(BF16)
    wkv_t = jnp.transpose(w_in[:, :, a:3 * a], (0, 2, 1)).astype(BF16)
    pw2_bf = conv_pw2.astype(BF16)
    wo_bf = w_out.astype(BF16)
    zero_hist = jnp.zeros((bp, CONV_HIST, a), F32)

    cos_rows, sin_rows = (jnp.tile(tab, (bs, 1)) for tab in tab_s[:2])
    hp, hs = x_prompt, x_sample.reshape(bs * ss, d)
    kv_prev = None
    cp_l, ks_l, vs_l, cs_l = [], [], [], []
    for l in range(depth):
        final = l == depth - 1
        conv_prm = (conv_dw_w[l], conv_dw_b[l], conv_ln_g[l], conv_ln_b[l], pw2_bf[l], conv_out_gain[l])
        mod_p, mod_s = mod[l, :bp], mod[l, bp:]

        q, kt_all, vt_all, ga, mc, cst = _inproj(hp, mod_p, norm_gain[l], wi_bf[l], tab_p, zero_hist, *conv_prm,
                                                 tm=512, wkv_t=wkv_t[l], layer=l, depth=depth, kv_prev=kv_prev)
        kv_prev = (kt_all, vt_all)
        hp = _moba_prompt(q, kt_all, vt_all, l, ga, mc, hp, mod_p, attn_out_gain[l], wo_bf[l], final_gain,
                          final=final)
        cp_l.append(cst)

        per_row = lambda m: jnp.repeat(m, ss, axis=0)
        q, k, v, ga, mc, cst = _inproj_rows(hs, per_row(mod_s[:, 0]), per_row(mod_s[:, 1]), norm_gain[l], wi_bf[l],
                                            cos_rows, sin_rows, state_conv[l], *conv_prm)
        seqs = lambda m: m.reshape(bs, ss, a)
        att = _paged_attention(seqs(q), seqs(k), seqs(v), ckt, cvt, page_table, l)
        hs = _outproj(att.reshape(bs * ss, a), ga, mc, hs, per_row(mod_s[:, 2]), attn_out_gain[l], wo_bf[l],
                      final_gain, final=final)
        ks_l.append(k.reshape(bs, ss, N_HEADS, HEAD_DIM))
        vs_l.append(v.reshape(bs, ss, N_HEADS, HEAD_DIM))
        cs_l.append(cst)

    kt_all, vt_all = kv_prev
    return (hp, hs.reshape(bs, ss, d), _heads_last(kt_all, bp, sp), _heads_last(vt_all, bp, sp), jnp.stack(cp_l),
            jnp.stack(ks_l), jnp.stack(vs_l), jnp.stack(cs_l))
```
